```python
import jax, jax.numpy as jnp
from jax import lax
import numpy as np

D_MODEL = 1024
BATCH = 16
SEQ = 2048
DEPTH = 1
DEC_BATCH = 4
DEC_SEQ = 8192
PAST_LEN = 128

N_HEADS = 16
N_KV_HEADS = 4
HEAD_DIM = 64
ATTN_WIDTH = N_HEADS * HEAD_DIM
KV_WIDTH = N_KV_HEADS * HEAD_DIM
WINDOW = 128
BLOCK = 128
ROPE_THETA = 10000.0
POOL_WIDTH = D_MODEL
POOL_WINDOWS = (2, 4, 8, 16)
N_POOL_GROUPS = 4
POOL_GROUP = POOL_WIDTH // N_POOL_GROUPS
N_BRANCHES = 2
RMS_EPS = 1e-6
NEG_INF = -1e30
IN_WIDTH = 2 * POOL_WIDTH + 2 * ATTN_WIDTH + 2 * KV_WIDTH + N_BRANCHES * D_MODEL

kernel_name = "hybrid_pool_swa_gated_encoder"


def rmsnorm(x, g):
    xf = x.astype(jnp.float32)
    y = xf * lax.rsqrt(jnp.mean(xf * xf, axis=-1, keepdims=True) + RMS_EPS) * g.astype(jnp.float32)
    return y.astype(x.dtype)


def rope(x, pos):
    half = HEAD_DIM // 2
    inv = ROPE_THETA ** (-jnp.arange(half, dtype=jnp.float32) / half)
    ang = pos.astype(jnp.float32)[:, None] * inv[None, :]
    cos = jnp.cos(ang)[None, :, None, :]
    sin = jnp.sin(ang)[None, :, None, :]
    xf = x.astype(jnp.float32)
    x1, x2 = xf[..., :half], xf[..., half:]
    out = jnp.concatenate([x1 * cos - x2 * sin, x2 * cos + x1 * sin], axis=-1)
    return out.astype(x.dtype)


def multiscale_pool(u):
    B, S, _ = u.shape
    uf = u.astype(jnp.float32).reshape(B, S, N_POOL_GROUPS, POOL_GROUP)
    c = jnp.concatenate([jnp.zeros((B, 1, N_POOL_GROUPS, POOL_GROUP), jnp.float32),
                         jnp.cumsum(uf, axis=1)], axis=1)
    t = jnp.arange(S)
    outs = []
    for g, w in enumerate(POOL_WINDOWS):
        lo = jnp.clip(t - w // 2, 0, S)
        hi = jnp.clip(t - w // 2 + w, 0, S)
        cg = c[:, :, g]
        mean = (cg[:, hi] - cg[:, lo]) / (hi - lo).astype(jnp.float32)[None, :, None]
        outs.append(mean)
    return jnp.stack(outs, axis=2) - uf


def windowed_attention(q, k, v, sink):
    B, S = q.shape[0], q.shape[1]
    nb = S // BLOCK
    G = N_HEADS // N_KV_HEADS
    span = BLOCK + 2 * WINDOW
    qg = jnp.moveaxis(q.reshape(B, nb, BLOCK, N_KV_HEADS, G, HEAD_DIM), 1, 0)
    pad = ((0, 0), (WINDOW, WINDOW), (0, 0), (0, 0))
    kp = jnp.pad(k, pad)
    vp = jnp.pad(v, pad)
    rel = jnp.arange(BLOCK)[:, None] - (jnp.arange(span)[None, :] - WINDOW)
    band = jnp.abs(rel) <= WINDOW
    scale = HEAD_DIM ** -0.5
    sink_b = sink.astype(jnp.float32).reshape(1, N_KV_HEADS, G, 1, 1)

    def one_block(args):
        qb, i = args
        start = i * BLOCK
        kb = lax.dynamic_slice_in_dim(kp, start, span, axis=1)
        vb = lax.dynamic_slice_in_dim(vp, start, span, axis=1)
        kpos = start - WINDOW + jnp.arange(span)
        valid = band & ((kpos >= 0) & (kpos < S))[None, :]
        s = jnp.einsum('bqhgd,bkhd->bhgqk', qb, kb).astype(jnp.float32) * scale
        s = jnp.where(valid, s, NEG_INF)
        logits = jnp.concatenate([s, jnp.broadcast_to(sink_b, s.shape[:-1] + (1,))], axis=-1)
        p = jax.nn.softmax(logits, axis=-1)[..., :span]
        return jnp.einsum('bhgqk,bkhd->bqhgd', p.astype(vb.dtype), vb)

    o = lax.map(one_block, (qg, jnp.arange(nb)))
    return jnp.moveaxis(o, 0, 1).reshape(B, S, ATTN_WIDTH)


def encoder_layer(x, norm_pre, w_in, w_pool_group, pool_scale, w_pool_proj,
                  attn_sink, w_attn_proj, w_out, norm_post):
    B, S, _ = x.shape
    h = rmsnorm(x, norm_pre)
    z = h @ w_in
    cuts = np.cumsum([POOL_WIDTH, POOL_WIDTH, ATTN_WIDTH, KV_WIDTH, KV_WIDTH, ATTN_WIDTH]).tolist()
    pool_u, pool_g, q, k, v, attn_g, merge = jnp.split(z, cuts, axis=-1)

    pooled = multiscale_pool(pool_u).astype(x.dtype)
    pb = jnp.einsum('bsgc,gcd->bsgd', pooled, w_pool_group).reshape(B, S, POOL_WIDTH)
    pb = pb * pool_scale * jax.nn.silu(pool_g)

    pos = jnp.arange(S)
    q = rope(q.reshape(B, S, N_HEADS, HEAD_DIM), pos)
    k = rope(k.reshape(B, S, N_KV_HEADS, HEAD_DIM), pos)
    v = v.reshape(B, S, N_KV_HEADS, HEAD_DIM)
    ab = windowed_attention(q, k, v, attn_sink) * jax.nn.silu(attn_g)

    gates = jax.nn.sigmoid(merge.astype(jnp.float32)).astype(x.dtype)
    gate_pool, gate_attn = gates[..., :D_MODEL], gates[..., D_MODEL:]
    m = gate_pool * (pb @ w_pool_proj) + gate_attn * (ab @ w_attn_proj)
    out = m @ w_out
    return x + rmsnorm(out, norm_post)


def setup_inputs(seed: int = 0) -> dict:
    key = jax.random.key(seed)
    ks = jax.random.split(key, 12)
    f32 = jnp.float32
    nrm = lambda k, shape, s: jax.random.normal(k, shape, f32) * s
    return {
        "x_prompt": nrm(ks[0], (BATCH, SEQ, D_MODEL), 1.0),
        "x_sample": nrm(ks[1], (DEC_BATCH, DEC_SEQ, D_MODEL), 1.0),
        "norm_pre": 1.0 + nrm(ks[2], (DEPTH, D_MODEL), 0.05),
        "w_in": nrm(ks[3], (DEPTH, D_MODEL, IN_WIDTH), D_MODEL ** -0.5),
        "w_pool_group": nrm(ks[4], (DEPTH, N_POOL_GROUPS, POOL_GROUP, POOL_GROUP), POOL_GROUP ** -0.5),
        "pool_scale": 1.0 + nrm(ks[5], (DEPTH, POOL_WIDTH), 0.1),
        "w_pool_proj": nrm(ks[6], (DEPTH, POOL_WIDTH, D_MODEL), POOL_WIDTH ** -0.5),
        "attn_sink": nrm(ks[7], (DEPTH, N_HEADS), 1.0),
        "w_attn_proj": nrm(ks[8], (DEPTH, ATTN_WIDTH, D_MODEL), ATTN_WIDTH ** -0.5),
        "w_out": nrm(ks[9], (DEPTH, D_MODEL, D_MODEL), D_MODEL ** -0.5),
        "norm_post": 1.0 + nrm(ks[10], (DEPTH, D_MODEL), 0.05),
    }


def reference(x_prompt, x_sample, norm_pre, w_in, w_pool_group, pool_scale, w_pool_proj,
              attn_sink, w_attn_proj, w_out, norm_post):
    y_prompt = x_prompt
    y_sample = x_sample
    for l in range(DEPTH):
        y_prompt = encoder_layer(y_prompt, norm_pre[l], w_in[l], w_pool_group[l], pool_scale[l],
                                 w_pool_proj[l], attn_sink[l], w_attn_proj[l], w_out[l], norm_post[l])
        y_sample = encoder_layer(y_sample, norm_pre[l], w_in[l], w_pool_group[l], pool_scale[l],
                                 w_pool_proj[l], attn_sink[l], w_attn_proj[l], w_out[l], norm_post[l])
    return (y_prompt, y_sample)
```

```python
import functools

import jax
import jax.numpy as jnp
from jax import lax
from jax.experimental import pallas as pl
from jax.experimental.pallas import tpu as pltpu

D_MODEL = 1024
N_HEADS = 16
N_KV_HEADS = 4
HEAD_DIM = 64
GROUP = N_HEADS // N_KV_HEADS
ATTN_WIDTH = N_HEADS * HEAD_DIM
KV_WIDTH = N_KV_HEADS * HEAD_DIM
WINDOW = 128
BLOCK = 128
SPAN = BLOCK + 2 * WINDOW
ROPE_THETA = 10000.0
POOL_WIDTH = D_MODEL
POOL_WINDOWS = (2, 4, 8, 16)
N_POOL_GROUPS = 4
POOL_GROUP = POOL_WIDTH // N_POOL_GROUPS
RMS_EPS = 1e-6
NEG_INF = -1e30

C_U = 0
C_PG = C_U + POOL_WIDTH
C_Q = C_PG + POOL_WIDTH
C_K = C_Q + ATTN_WIDTH
C_V = C_K + KV_WIDTH
C_AG = C_V + KV_WIDTH
C_MG = C_AG + ATTN_WIDTH
IN_WIDTH = C_MG + 2 * D_MODEL

LANES = 128
HALF_LANES = LANES // 2
BF16_ROWS = 16
POOL_HALO = 8
VMEM_LIMIT_BYTES = 56 * 1024 * 1024

TM_INPROJ = 512
TQ_MIX = 256
COL_CHUNK = 512


def _silu(z):
    return z * jax.nn.sigmoid(z)


def _inproj_kernel(x_ref, g_ref, w_ref, cos_ref, sin_ref,
                   u_ref, sg_ref, q_ref, ag_ref, gate_ref, kt_ref, va_ref, vb_ref):
    tm = x_ref.shape[0]
    x = x_ref[...]
    ms = jnp.mean(x * x, axis=-1, keepdims=True)
    h = (x * lax.rsqrt(ms + RMS_EPS) * g_ref[...]).astype(jnp.bfloat16)

    def proj(lo, width):
        return jnp.dot(h, w_ref[:, lo:lo + width], preferred_element_type=jnp.float32)

    lane = lax.broadcasted_iota(jnp.int32, (tm, LANES), 1)
    low_half = lane < HALF_LANES
    rot_first = (lane % HEAD_DIM) < (HEAD_DIM // 2)
    cos = cos_ref[...]
    sin = sin_ref[...]

    def rope(zb):
        partner = jnp.where(rot_first, pltpu.roll(zb, LANES - HEAD_DIM // 2, 1),
                            pltpu.roll(zb, HEAD_DIM // 2, 1))
        return zb * cos + partner * sin

    for c in range(0, POOL_WIDTH, COL_CHUNK):
        u_ref[:, c:c + COL_CHUNK] = proj(C_U + c, COL_CHUNK).astype(u_ref.dtype)
    for c in range(0, POOL_WIDTH, COL_CHUNK):
        sg_ref[:, c:c + COL_CHUNK] = _silu(proj(C_PG + c, COL_CHUNK)).astype(sg_ref.dtype)
    for c in range(0, ATTN_WIDTH, COL_CHUNK):
        z = proj(C_Q + c, COL_CHUNK)
        for j in range(COL_CHUNK // LANES):
            zb = rope(z[:, j * LANES:(j + 1) * LANES]) * (HEAD_DIM ** -0.5)
            q_ref[:, c + j * LANES:c + (j + 1) * LANES] = zb.astype(q_ref.dtype)
    for c in range(0, ATTN_WIDTH, COL_CHUNK):
        ag_ref[:, c:c + COL_CHUNK] = _silu(proj(C_AG + c, COL_CHUNK)).astype(ag_ref.dtype)
    for c in range(0, 2 * D_MODEL, COL_CHUNK):
        gate_ref[:, c:c + COL_CHUNK] = jax.nn.sigmoid(proj(C_MG + c, COL_CHUNK)).astype(gate_ref.dtype)

    zk = proj(C_K, KV_WIDTH)
    for j in range(KV_WIDTH // LANES):
        blk = rope(zk[:, j * LANES:(j + 1) * LANES])
        swapped = pltpu.roll(blk, HALF_LANES, 1)
        even = jnp.where(low_half, blk, swapped)
        odd = jnp.where(low_half, swapped, blk)
        kt_ref[(2 * j) * LANES:(2 * j + 1) * LANES, :] = even.T.astype(kt_ref.dtype)
        kt_ref[(2 * j + 1) * LANES:(2 * j + 2) * LANES, :] = odd.T.astype(kt_ref.dtype)

    zv = proj(C_V, KV_WIDTH)
    one_mid = jnp.where(lane == HALF_LANES, 1.0, 0.0)
    one_first = jnp.where(lane == 0, 1.0, 0.0)
    for j in range(KV_WIDTH // LANES):
        blk = zv[:, j * LANES:(j + 1) * LANES]
        swapped = pltpu.roll(blk, HALF_LANES, 1)
        va_ref[:, (2 * j) * LANES:(2 * j + 1) * LANES] = jnp.where(low_half, blk, one_mid).astype(va_ref.dtype)
        va_ref[:, (2 * j + 1) * LANES:(2 * j + 2) * LANES] = jnp.where(low_half, swapped, one_mid).astype(va_ref.dtype)
        vb_ref[:, (2 * j) * LANES:(2 * j + 1) * LANES] = jnp.where(low_half, one_first, swapped).astype(vb_ref.dtype)
        vb_ref[:, (2 * j + 1) * LANES:(2 * j + 2) * LANES] = jnp.where(low_half, one_first, blk).astype(vb_ref.dtype)


def _inproj_call(x, g_pre, w_in, cos, sin, *, tm):
    batch, seq, _ = x.shape
    nst = seq // tm
    bf16 = jnp.bfloat16
    row_spec = lambda width: pl.BlockSpec((None, tm, width), lambda b, i: (b, i, 0))
    const = lambda shape: pl.BlockSpec(shape, lambda b, i: (0,) * len(shape))
    out_shape = (
        jax.ShapeDtypeStruct((batch, seq, POOL_WIDTH), bf16),
        jax.ShapeDtypeStruct((batch, seq, POOL_WIDTH), bf16),
        jax.ShapeDtypeStruct((batch, seq, ATTN_WIDTH), bf16),
        jax.ShapeDtypeStruct((batch, seq, ATTN_WIDTH), bf16),
        jax.ShapeDtypeStruct((batch, seq, 2 * D_MODEL), bf16),
        jax.ShapeDtypeStruct((batch, N_KV_HEADS * LANES, seq), bf16),
        jax.ShapeDtypeStruct((batch, seq, N_KV_HEADS * LANES), bf16),
        jax.ShapeDtypeStruct((batch, seq, N_KV_HEADS * LANES), bf16),
    )
    out_specs = (
        row_spec(POOL_WIDTH), row_spec(POOL_WIDTH), row_spec(ATTN_WIDTH), row_spec(ATTN_WIDTH),
        row_spec(2 * D_MODEL),
        pl.BlockSpec((None, N_KV_HEADS * LANES, tm), lambda b, i: (b, 0, i)),
        row_spec(N_KV_HEADS * LANES), row_spec(N_KV_HEADS * LANES),
    )
    in_specs = [
        row_spec(D_MODEL),
        const((1, D_MODEL)),
        const((D_MODEL, IN_WIDTH)),
        pl.BlockSpec((tm, LANES), lambda b, i: (i, 0)),
        pl.BlockSpec((tm, LANES), lambda b, i: (i, 0)),
    ]
    return pl.pallas_call(
        _inproj_kernel,
        grid=(batch, nst),
        in_specs=in_specs,
        out_specs=out_specs,
        out_shape=out_shape,
        compiler_params=pltpu.CompilerParams(
            dimension_semantics=("arbitrary", "arbitrary"),
            vmem_limit_bytes=VMEM_LIMIT_BYTES),
        name="inproj",
    )(x, g_pre, w_in, cos, sin)


def _mixer_kernel(x_ref, u_ref, up_ref, un_ref, sg_ref, q_ref, ag_ref, gate_ref,
                  kt_ref, ktp_ref, ktn_ref, va_ref, vap_ref, van_ref, vb_ref, vbp_ref, vbn_ref,
                  wpg_ref, ps_ref, wpp_ref, sink_ref, wap_ref, wo_ref, gpost_ref,
                  y_ref,
                  kfull, vafull, vbfull, uext, ab_ref, pb_ref, *, seq):
    tq = x_ref.shape[0]
    nqb = tq // BLOCK
    i = pl.program_id(1)
    nt = pl.num_programs(1)
    is_first = i == 0
    is_last = i == nt - 1

    kfull[:, 0:WINDOW] = ktp_ref[...]
    kfull[:, WINDOW:WINDOW + tq] = kt_ref[...]
    kfull[:, WINDOW + tq:] = ktn_ref[...]
    vafull[0:WINDOW, :] = vap_ref[...]
    vafull[WINDOW:WINDOW + tq, :] = va_ref[...]
    vafull[WINDOW + tq:, :] = van_ref[...]
    vbfull[0:WINDOW, :] = vbp_ref[...]
    vbfull[WINDOW:WINDOW + tq, :] = vb_ref[...]
    vbfull[WINDOW + tq:, :] = vbn_ref[...]

    qrow = lax.broadcasted_iota(jnp.int32, (BLOCK, SPAN), 0)
    kcol = lax.broadcasted_iota(jnp.int32, (BLOCK, SPAN), 1)
    rel = kcol - qrow
    band = (rel >= 0) & (rel <= 2 * WINDOW)
    lane = lax.broadcasted_iota(jnp.int32, (BLOCK, LANES), 1)
    low_half = lane < HALF_LANES

    for qi in range(nqb):
        rows = slice(qi * BLOCK, (qi + 1) * BLOCK)
        mask = band
        if qi == 0:
            mask = mask & ((kcol >= WINDOW) | jnp.logical_not(is_first))
        if qi == nqb - 1:
            mask = mask & ((kcol < WINDOW + BLOCK) | jnp.logical_not(is_last))
        for pair in range(N_HEADS // 2):
            hkv = (2 * pair) // GROUP
            qb = q_ref[rows, pair * LANES:(pair + 1) * LANES]
            kd = kfull[hkv * LANES:(hkv + 1) * LANES, qi * BLOCK:qi * BLOCK + SPAN]
            merged = None
            for parity in range(2):
                head = 2 * pair + parity
                keep = low_half if parity == 0 else jnp.logical_not(low_half)
                qm = jnp.where(keep, qb, jnp.zeros_like(qb))
                s = jnp.dot(qm, kd, preferred_element_type=jnp.float32)
                s = jnp.where(mask, s, NEG_INF)
                sink = sink_ref[head:head + 1, 0:1]
                m = jnp.maximum(jnp.max(s, axis=-1, keepdims=True), sink)
                p = jnp.exp(s - m).astype(jnp.bfloat16)
                vfull = vafull if parity == 0 else vbfull
                vext = vfull[qi * BLOCK:qi * BLOCK + SPAN, hkv * LANES:(hkv + 1) * LANES]
                o = jnp.dot(p, vext, preferred_element_type=jnp.float32)
                ones_lane = HALF_LANES if parity == 0 else 0
                denom = o[:, ones_lane:ones_lane + 1] + jnp.exp(sink - m)
                o = o / denom
                merged = o if parity == 0 else jnp.where(low_half, merged, o)
            gated = merged * ag_ref[rows, pair * LANES:(pair + 1) * LANES].astype(jnp.float32)
            ab_ref[rows, pair * LANES:(pair + 1) * LANES] = gated.astype(ab_ref.dtype)

    up = up_ref[...].astype(jnp.float32)[BF16_ROWS - POOL_HALO:, :]
    un = un_ref[...].astype(jnp.float32)[:POOL_HALO, :]
    uext[0:POOL_HALO, :] = jnp.where(is_first, 0.0, up)
    uext[POOL_HALO:POOL_HALO + tq, :] = u_ref[...].astype(jnp.float32)
    uext[POOL_HALO + tq:, :] = jnp.where(is_last, 0.0, un)
    pos = i * tq + lax.broadcasted_iota(jnp.int32, (tq, 1), 0)
    for g, w in enumerate(POOL_WINDOWS):
        cols = slice(g * POOL_GROUP, (g + 1) * POOL_GROUP)
        lo = jnp.maximum(pos - w // 2, 0)
        hi = jnp.minimum(pos - w // 2 + w, seq)
        cnt = (hi - lo).astype(jnp.float32)
        acc = uext[POOL_HALO - w // 2:POOL_HALO - w // 2 + tq, cols]
        for d in range(-w // 2 + 1, w // 2):
            acc = acc + uext[POOL_HALO + d:POOL_HALO + d + tq, cols]
        pooled = acc / cnt - uext[POOL_HALO:POOL_HALO + tq, cols]
        pbg = jnp.dot(pooled.astype(jnp.bfloat16), wpg_ref[g], preferred_element_type=jnp.float32)
        pbg = pbg * ps_ref[:, cols] * sg_ref[:, cols].astype(jnp.float32)
        pb_ref[:, cols] = pbg.astype(pb_ref.dtype)

    pp = jnp.dot(pb_ref[...], wpp_ref[...], preferred_element_type=jnp.float32)
    ap = jnp.dot(ab_ref[...], wap_ref[...], preferred_element_type=jnp.float32)
    merged = (gate_ref[:, :D_MODEL].astype(jnp.float32) * pp
              + gate_ref[:, D_MODEL:].astype(jnp.float32) * ap)
    out = jnp.dot(merged.astype(jnp.bfloat16), wo_ref[...], preferred_element_type=jnp.float32)
    ms = jnp.mean(out * out, axis=-1, keepdims=True)
    y_ref[...] = x_ref[...] + out * lax.rsqrt(ms + RMS_EPS) * gpost_ref[...]


def _mixer_call(x, u, sg, q, ag, gates, kt, va, vb, wpg, ps, wpp, sink, wap, wo, g_post, *, tq):
    batch, seq, _ = x.shape
    nt = seq // tq
    kvw = N_KV_HEADS * LANES
    row_spec = lambda width: pl.BlockSpec((None, tq, width), lambda b, i: (b, i, 0))
    const = lambda shape: pl.BlockSpec(shape, lambda b, i: (0,) * len(shape))

    def halo_rows(rows, width, side):
        per = tq // rows
        last = seq // rows - 1
        if side < 0:
            return pl.BlockSpec((None, rows, width), lambda b, i: (b, jnp.maximum(i * per - 1, 0), 0))
        return pl.BlockSpec((None, rows, width), lambda b, i: (b, jnp.minimum((i + 1) * per, last), 0))

    per_k = tq // WINDOW
    last_k = seq // WINDOW - 1
    in_specs = [
        row_spec(D_MODEL),
        row_spec(POOL_WIDTH),
        halo_rows(BF16_ROWS, POOL_WIDTH, -1),
        halo_rows(BF16_ROWS, POOL_WIDTH, +1),
        row_spec(POOL_WIDTH),
        row_spec(ATTN_WIDTH),
        row_spec(ATTN_WIDTH),
        row_spec(2 * D_MODEL),
        pl.BlockSpec((None, kvw, tq), lambda b, i: (b, 0, i)),
        pl.BlockSpec((None, kvw, WINDOW), lambda b, i: (b, 0, jnp.maximum(i * per_k - 1, 0))),
        pl.BlockSpec((None, kvw, WINDOW), lambda b, i: (b, 0, jnp.minimum((i + 1) * per_k, last_k))),
        row_spec(kvw), halo_rows(WINDOW, kvw, -1), halo_rows(WINDOW, kvw, +1),
        row_spec(kvw), halo_rows(WINDOW, kvw, -1), halo_rows(WINDOW, kvw, +1),
        const((N_POOL_GROUPS, POOL_GROUP, POOL_GROUP)),
        const((1, POOL_WIDTH)),
        const((POOL_WIDTH, D_MODEL)),
        const((N_HEADS, LANES)),
        const((ATTN_WIDTH, D_MODEL)),
        const((D_MODEL, D_MODEL)),
        const((1, D_MODEL)),
    ]
    scratch_shapes = [
        pltpu.VMEM((kvw, tq + 2 * WINDOW), jnp.bfloat16),
        pltpu.VMEM((tq + 2 * WINDOW, kvw), jnp.bfloat16),
        pltpu.VMEM((tq + 2 * WINDOW, kvw), jnp.bfloat16),
        pltpu.VMEM((tq + 2 * POOL_HALO, POOL_WIDTH), jnp.float32),
        pltpu.VMEM((tq, ATTN_WIDTH), jnp.bfloat16),
        pltpu.VMEM((tq, POOL_WIDTH), jnp.bfloat16),
    ]
    return pl.pallas_call(
        functools.partial(_mixer_kernel, seq=seq),
        grid=(batch, nt),
        in_specs=in_specs,
        out_specs=row_spec(D_MODEL),
        out_shape=jax.ShapeDtypeStruct((batch, seq, D_MODEL), jnp.float32),
        scratch_shapes=scratch_shapes,
        compiler_params=pltpu.CompilerParams(
            dimension_semantics=("arbitrary", "arbitrary"),
            vmem_limit_bytes=VMEM_LIMIT_BYTES),
        name="mixer",
    )(x, u, u, u, sg, q, ag, gates, kt, kt, kt, va, va, va, vb, vb, vb,
      wpg, ps, wpp, sink, wap, wo, g_post)


def _rope_tables(seq):
    half = HEAD_DIM // 2
    inv = ROPE_THETA ** (-jnp.arange(half, dtype=jnp.float32) / half)
    ang = jnp.arange(seq).astype(jnp.float32)[:, None] * inv[None, :]
    cos = jnp.cos(ang)
    sin = jnp.sin(ang)
    reps = LANES // HEAD_DIM
    cos_t = jnp.tile(jnp.concatenate([cos, cos], axis=-1), (1, reps))
    sin_t = jnp.tile(jnp.concatenate([-sin, sin], axis=-1), (1, reps))
    return cos_t, sin_t


def _encoder_layer(x, norm_pre, w_in, w_pool_group, pool_scale, w_pool_proj,
                   attn_sink, w_attn_proj, w_out, norm_post, *, tm=TM_INPROJ, tq=TQ_MIX):
    batch, seq, _ = x.shape
    tm = min(tm, seq)
    tq = min(tq, seq)
    assert seq % tm == 0 and seq % tq == 0 and tq % BLOCK == 0 and tm % LANES == 0
    bf16 = jnp.bfloat16
    cos_t, sin_t = _rope_tables(seq)
    u, sg, q, ag, gates, kt, va, vb = _inproj_call(
        x, norm_pre.reshape(1, D_MODEL), w_in.astype(bf16), cos_t, sin_t, tm=tm)
    sink = jnp.broadcast_to(attn_sink.astype(jnp.float32)[:, None], (N_HEADS, LANES))
    return _mixer_call(
        x, u, sg, q, ag, gates, kt, va, vb,
        w_pool_group.astype(bf16), pool_scale.reshape(1, POOL_WIDTH).astype(jnp.float32),
        w_pool_proj.astype(bf16), sink, w_attn_proj.astype(bf16), w_out.astype(bf16),
        norm_post.reshape(1, D_MODEL), tq=tq)


def kernel(x_prompt, x_sample, norm_pre, w_in, w_pool_group, pool_scale, w_pool_proj,
           attn_sink, w_attn_proj, w_out, norm_post):
    y_prompt, y_sample = x_prompt, x_sample
    for l in range(norm_pre.shape[0]):
        layer = functools.partial(
            _encoder_layer, norm_pre=norm_pre[l], w_in=w_in[l], w_pool_group=w_pool_group[l],
            pool_scale=pool_scale[l], w_pool_proj=w_pool_proj[l], attn_sink=attn_sink[l],
            w_attn_proj=w_attn_proj[l], w_out=w_out[l], norm_post=norm_post[l])
        y_prompt = layer(y_prompt)
        y_sample = layer(y_sample)
    return (y_prompt, y_sample)
```

```python
import functools
import math

import jax
import jax.numpy as jnp
from jax import lax
from jax.experimental import pallas as pl
from jax.experimental.pallas import tpu as pltpu

D_MODEL = 1024
N_HEADS = 16
N_KV_HEADS = 4
HEAD_DIM = 64
GROUP = N_HEADS // N_KV_HEADS
ATTN_WIDTH = N_HEADS * HEAD_DIM
KV_WIDTH = N_KV_HEADS * HEAD_DIM
WINDOW = 128
BLOCK = 128
SPAN = BLOCK + 2 * WINDOW
ROPE_THETA = 10000.0
POOL_WIDTH = D_MODEL
POOL_WINDOWS = (2, 4, 8, 16)
N_POOL_GROUPS = 4
POOL_GROUP = POOL_WIDTH // N_POOL_GROUPS
RMS_EPS = 1e-6
NEG_INF = -1e30
LOG2_E = math.log2(math.e)

C_U = 0
C_PG = C_U + POOL_WIDTH
C_Q = C_PG + POOL_WIDTH
C_K = C_Q + ATTN_WIDTH
C_V = C_K + KV_WIDTH
C_AG = C_V + KV_WIDTH
C_MG = C_AG + ATTN_WIDTH
IN_WIDTH = C_MG + 2 * D_MODEL

LANES = 128
HALF_LANES = LANES // 2
SUBLANES = 8
POOL_HALO = 8
POOL_PAD = 3 * SUBLANES
KV_SLOT = 2 * LANES
VMEM_LIMIT_BYTES = 56 * 1024 * 1024

TM_INPROJ = 512
TQ_MIX = 256
COL_CHUNK = 512

assert POOL_WINDOWS == (2, 4, 8, 16) and POOL_HALO == max(POOL_WINDOWS) // 2


def _silu(z):
    return z * jax.nn.sigmoid(z)


def _rms_scale(x, g):
    ms = jnp.mean(x * x, axis=-1, keepdims=True)
    return x * lax.rsqrt(ms + RMS_EPS) * g


def _inproj_kernel(x_ref, xp_ref, xn_ref, g_ref, w_ref, cos_ref, sin_ref, wpg_ref, ps_ref,
                   pb_ref, q_ref, ag_ref, gate_ref, kt_ref, vl_ref, vh_ref,
                   uext, *, seq):
    tm = x_ref.shape[0]
    i = pl.program_id(1)
    is_first = i == 0
    is_last = i == pl.num_programs(1) - 1
    g_pre = g_ref[...]
    hf = _rms_scale(x_ref[...], g_pre)
    h = hf.astype(jnp.bfloat16)

    def proj(lo, width):
        return jnp.dot(h, w_ref[:, lo:lo + width], preferred_element_type=jnp.float32)

    lane = lax.broadcasted_iota(jnp.int32, (tm, LANES), 1)
    low_half = lane < HALF_LANES
    rot_first = (lane % HEAD_DIM) < (HEAD_DIM // 2)
    cos = cos_ref[...]
    sin = sin_ref[...]

    def rope(zb):
        partner = jnp.where(rot_first, pltpu.roll(zb, LANES - HEAD_DIM // 2, 1),
                            pltpu.roll(zb, HEAD_DIM // 2, 1))
        return zb * cos + partner * sin

    h_ext = jnp.concatenate(
        [_rms_scale(xp_ref[...], g_pre), hf, _rms_scale(xn_ref[...], g_pre)], axis=0).astype(jnp.bfloat16)
    ext = tm + 2 * POOL_HALO
    for c in range(0, POOL_WIDTH, COL_CHUNK):
        uext[0:ext, c:c + COL_CHUNK] = jnp.dot(h_ext, w_ref[:, C_U + c:C_U + c + COL_CHUNK],
                                               preferred_element_type=jnp.float32)
    zero_halo = jnp.zeros((POOL_HALO, POOL_WIDTH), jnp.float32)
    uext[0:POOL_HALO, :] = jnp.where(is_first, zero_halo, uext[0:POOL_HALO, :])
    uext[ext - POOL_HALO:ext, :] = jnp.where(is_last, zero_halo, uext[ext - POOL_HALO:ext, :])
    uext[ext:ext + POOL_PAD, :] = jnp.zeros((POOL_PAD, POOL_WIDTH), jnp.float32)

    pos = i * tm + lax.broadcasted_iota(jnp.int32, (tm, LANES), 0)
    for g, w in enumerate(POOL_WINDOWS):
        cols = slice(g * POOL_GROUP, (g + 1) * POOL_GROUP)
        u_all = uext[:, cols]
        center = u_all[POOL_HALO:POOL_HALO + tm]
        n1 = tm + 2 * POOL_HALO + SUBLANES
        win2 = u_all[0:n1] + u_all[1:n1 + 1]
        if w == 2:
            total = win2[POOL_HALO - 1:POOL_HALO - 1 + tm]
        elif w == 4:
            total = win2[POOL_HALO - 2:POOL_HALO - 2 + tm] + win2[POOL_HALO:POOL_HALO + tm]
        else:
            n2 = tm + 2 * POOL_HALO
            win4 = win2[0:n2] + win2[2:n2 + 2]
            if w == 8:
                total = win4[POOL_HALO - 4:POOL_HALO - 4 + tm] + win4[POOL_HALO:POOL_HALO + tm]
            else:
                n3 = tm + POOL_HALO
                win8 = win4[0:n3] + win4[4:n3 + 4]
                total = win8[0:tm] + win8[POOL_HALO:POOL_HALO + tm]
        lo = jnp.maximum(pos - w // 2, 0)
        hi = jnp.minimum(pos - w // 2 + w, seq)
        inv_cnt = 1.0 / (hi - lo).astype(jnp.float32)
        inv_cnt = jnp.concatenate([inv_cnt] * (POOL_GROUP // LANES), axis=-1)
        pooled = total * inv_cnt - center
        pbg = jnp.dot(pooled.astype(jnp.bfloat16), wpg_ref[g], preferred_element_type=jnp.float32)
        gate = _silu(proj(C_PG + g * POOL_GROUP, POOL_GROUP))
        pb_ref[:, cols] = (pbg * ps_ref[:, cols] * gate).astype(pb_ref.dtype)

    q_scale = (HEAD_DIM ** -0.5) * LOG2_E
    for c in range(0, ATTN_WIDTH, COL_CHUNK):
        z = proj(C_Q + c, COL_CHUNK)
        for j in range(COL_CHUNK // LANES):
            zb = rope(z[:, j * LANES:(j + 1) * LANES]) * q_scale
            q_ref[:, c + j * LANES:c + (j + 1) * LANES] = zb.astype(q_ref.dtype)
    for c in range(0, ATTN_WIDTH, COL_CHUNK):
        ag_ref[:, c:c + COL_CHUNK] = _silu(proj(C_AG + c, COL_CHUNK)).astype(ag_ref.dtype)
    for c in range(0, 2 * D_MODEL, COL_CHUNK):
        gate_ref[:, c:c + COL_CHUNK] = jax.nn.sigmoid(proj(C_MG + c, COL_CHUNK)).astype(gate_ref.dtype)

    def split_heads(blk):
        swapped = pltpu.roll(blk, HALF_LANES, 1)
        return ((jnp.where(low_half, blk, 0.0), jnp.where(low_half, 0.0, swapped)),
                (jnp.where(low_half, swapped, 0.0), jnp.where(low_half, 0.0, blk)))

    zk = proj(C_K, KV_WIDTH)
    for j in range(KV_WIDTH // LANES):
        for par, (lowv, highv) in enumerate(split_heads(rope(zk[:, j * LANES:(j + 1) * LANES]))):
            base = (2 * j + par) * KV_SLOT
            kt_ref[base:base + LANES, :] = lowv.T.astype(kt_ref.dtype)
            kt_ref[base + LANES:base + KV_SLOT, :] = highv.T.astype(kt_ref.dtype)

    zv = proj(C_V, KV_WIDTH)
    ones_low = jnp.where(low_half, 1.0, 0.0).astype(vl_ref.dtype)
    ones_high = jnp.where(low_half, 0.0, 1.0).astype(vh_ref.dtype)
    for j in range(KV_WIDTH // LANES):
        for par, (lowv, highv) in enumerate(split_heads(zv[:, j * LANES:(j + 1) * LANES])):
            base = (2 * j + par) * KV_SLOT
            vl_ref[:, base:base + LANES] = lowv.astype(vl_ref.dtype)
            vl_ref[:, base + LANES:base + KV_SLOT] = ones_low
            vh_ref[:, base:base + LANES] = highv.astype(vh_ref.dtype)
            vh_ref[:, base + LANES:base + KV_SLOT] = ones_high


def _inproj_call(x, g_pre, w_in, cos, sin, wpg, ps, *, tm):
    batch, seq, _ = x.shape
    nst = seq // tm
    bf16 = jnp.bfloat16
    kvw = N_KV_HEADS * KV_SLOT
    row_spec = lambda width: pl.BlockSpec((None, tm, width), lambda b, i: (b, i, 0))
    const = lambda shape: pl.BlockSpec(shape, lambda b, i: (0,) * len(shape))
    per = tm // POOL_HALO
    last = seq // POOL_HALO - 1
    out_shape = (
        jax.ShapeDtypeStruct((batch, seq, POOL_WIDTH), bf16),
        jax.ShapeDtypeStruct((batch, seq, ATTN_WIDTH), bf16),
        jax.ShapeDtypeStruct((batch, seq, ATTN_WIDTH), bf16),
        jax.ShapeDtypeStruct((batch, seq, 2 * D_MODEL), bf16),
        jax.ShapeDtypeStruct((batch, kvw, seq), bf16),
        jax.ShapeDtypeStruct((batch, seq, kvw), bf16),
        jax.ShapeDtypeStruct((batch, seq, kvw), bf16),
    )
    out_specs = (
        row_spec(POOL_WIDTH), row_spec(ATTN_WIDTH), row_spec(ATTN_WIDTH), row_spec(2 * D_MODEL),
        pl.BlockSpec((None, kvw, tm), lambda b, i: (b, 0, i)),
        row_spec(kvw), row_spec(kvw),
    )
    in_specs = [
        row_spec(D_MODEL),
        pl.BlockSpec((None, POOL_HALO, D_MODEL), lambda b, i: (b, jnp.maximum(i * per - 1, 0), 0)),
        pl.BlockSpec((None, POOL_HALO, D_MODEL), lambda b, i: (b, jnp.minimum((i + 1) * per, last), 0)),
        const((1, D_MODEL)),
        const((D_MODEL, IN_WIDTH)),
        pl.BlockSpec((tm, LANES), lambda b, i: (i, 0)),
        pl.BlockSpec((tm, LANES), lambda b, i: (i, 0)),
        const((N_POOL_GROUPS, POOL_GROUP, POOL_GROUP)),
        const((1, POOL_WIDTH)),
    ]
    return pl.pallas_call(
        functools.partial(_inproj_kernel, seq=seq),
        grid=(batch, nst),
        in_specs=in_specs,
        out_specs=out_specs,
        out_shape=out_shape,
        scratch_shapes=[pltpu.VMEM((tm + 2 * POOL_HALO + POOL_PAD, POOL_WIDTH), jnp.float32)],
        compiler_params=pltpu.CompilerParams(
            dimension_semantics=("arbitrary", "arbitrary"),
            vmem_limit_bytes=VMEM_LIMIT_BYTES),
        name="inproj",
    )(x, x, x, g_pre, w_in, cos, sin, wpg, ps)


def _mixer_kernel(x_ref, pb_ref, q_ref, ag_ref, gate_ref,
                  kt_ref, ktp_ref, ktn_ref, vl_ref, vlp_ref, vln_ref, vh_ref, vhp_ref, vhn_ref,
                  wpp_ref, sink_ref, wap_ref, wo_ref, gpost_ref,
                  y_ref,
                  kfull, vlfull, vhfull, ab_ref):
    tq = x_ref.shape[0]
    nqb = tq // BLOCK
    i = pl.program_id(1)
    is_first = i == 0
    is_last = i == pl.num_programs(1) - 1

    kfull[:, 0:WINDOW] = ktp_ref[...]
    kfull[:, WINDOW:WINDOW + tq] = kt_ref[...]
    kfull[:, WINDOW + tq:] = ktn_ref[...]
    for full, prev, cur, nxt in ((vlfull, vlp_ref, vl_ref, vln_ref), (vhfull, vhp_ref, vh_ref, vhn_ref)):
        full[0:WINDOW, :] = prev[...]
        full[WINDOW:WINDOW + tq, :] = cur[...]
        full[WINDOW + tq:, :] = nxt[...]

    qrow = lax.broadcasted_iota(jnp.int32, (BLOCK, LANES), 0)
    kcol = lax.broadcasted_iota(jnp.int32, (BLOCK, LANES), 1)
    low_half = kcol < HALF_LANES

    def chunk_masks(qi):
        first = kcol >= qrow
        last = kcol <= qrow
        if qi == 0:
            first = first & jnp.logical_not(is_first)
        if qi == nqb - 1:
            last = last & jnp.logical_not(is_last)
        return first, last

    masks = [chunk_masks(qi) for qi in range(nqb)]
    n_chunks = SPAN // LANES

    def scores(qi, hkv):
        rows = slice(qi * BLOCK, (qi + 1) * BLOCK)
        keys = slice(qi * BLOCK, qi * BLOCK + SPAN)
        q2 = jnp.concatenate([q_ref[rows, (2 * hkv + r) * LANES:(2 * hkv + r + 1) * LANES]
                              for r in range(GROUP // 2)], axis=0)
        kw = jnp.concatenate([kfull[hkv * KV_SLOT:hkv * KV_SLOT + LANES, keys],
                              kfull[hkv * KV_SLOT + LANES:(hkv + 1) * KV_SLOT, keys]], axis=1)
        return jnp.dot(q2, kw, preferred_element_type=jnp.float32)

    def softmax(qi, hkv, s):
        first_ok, last_ok = masks[qi]
        p_rows, sink_terms = [], []
        for r in range(GROUP // 2):
            p_cols, sink_pair = [], []
            for c in range(2):
                head = hkv * GROUP + 2 * r + c
                blk = s[r * BLOCK:(r + 1) * BLOCK, c * SPAN:(c + 1) * SPAN]
                chunks = [blk[:, k * LANES:(k + 1) * LANES] for k in range(n_chunks)]
                chunks[0] = jnp.where(first_ok, chunks[0], NEG_INF)
                chunks[-1] = jnp.where(last_ok, chunks[-1], NEG_INF)
                top = functools.reduce(jnp.maximum, chunks)
                sink_row = sink_ref[head:head + 1, :]
                m = jnp.maximum(jnp.max(top, axis=-1, keepdims=True), sink_row)
                p_cols += [jnp.exp2(ch - m).astype(jnp.bfloat16) for ch in chunks]
                sink_pair.append(jnp.exp2(sink_row - m))
            p_rows.append(jnp.concatenate(p_cols, axis=-1))
            sink_terms.append(jnp.where(low_half, sink_pair[0], sink_pair[1]))
        return jnp.concatenate(p_rows, axis=0), sink_terms

    def weighted_values(qi, hkv, p):
        keys = slice(qi * BLOCK, qi * BLOCK + SPAN)
        slot = slice(hkv * KV_SLOT, (hkv + 1) * KV_SLOT)
        vw = jnp.concatenate([vlfull[keys, slot], vhfull[keys, slot]], axis=0)
        return jnp.dot(p, vw, preferred_element_type=jnp.float32)

    def finalize(qi, hkv, o, sink_terms):
        rows = slice(qi * BLOCK, (qi + 1) * BLOCK)
        for r in range(GROUP // 2):
            lanes = slice((2 * hkv + r) * LANES, (2 * hkv + r + 1) * LANES)
            values = o[r * BLOCK:(r + 1) * BLOCK, 0:LANES]
            denom = o[r * BLOCK:(r + 1) * BLOCK, LANES:KV_SLOT] + sink_terms[r]
            gated = values / denom * ag_ref[rows, lanes].astype(jnp.float32)
            ab_ref[rows, lanes] = gated.astype(ab_ref.dtype)

    items = [(qi, hkv) for qi in range(nqb) for hkv in range(N_KV_HEADS)]
    s_q, p_q, o_q = {}, {}, {}
    for t in range(len(items) + 3):
        if t < len(items):
            s_q[t] = scores(*items[t])
        if 0 <= t - 1 < len(items):
            p_q[t - 1] = softmax(*items[t - 1], s_q.pop(t - 1))
        if 0 <= t - 2 < len(items):
            p, sink_terms = p_q.pop(t - 2)
            o_q[t - 2] = (weighted_values(*items[t - 2], p), sink_terms)
        if 0 <= t - 3 < len(items):
            finalize(*items[t - 3], *o_q.pop(t - 3))

    pp = jnp.dot(pb_ref[...], wpp_ref[...], preferred_element_type=jnp.float32)
    ap = jnp.dot(ab_ref[...], wap_ref[...], preferred_element_type=jnp.float32)
    merged = (gate_ref[:, :D_MODEL].astype(jnp.float32) * pp
              + gate_ref[:, D_MODEL:].astype(jnp.float32) * ap)
    out = jnp.dot(merged.astype(jnp.bfloat16), wo_ref[...], preferred_element_type=jnp.float32)
    y_ref[...] = x_ref[...] + _rms_scale(out, gpost_ref[...])


def _mixer_call(x, pb, q, ag, gates, kt, vl, vh, wpp, sink, wap, wo, g_post, *, tq):
    batch, seq, _ = x.shape
    nt = seq // tq
    kvw = N_KV_HEADS * KV_SLOT
    row_spec = lambda width: pl.BlockSpec((None, tq, width), lambda b, i: (b, i, 0))
    const = lambda shape: pl.BlockSpec(shape, lambda b, i: (0,) * len(shape))
    per = tq // WINDOW
    last = seq // WINDOW - 1
    prev_blk = lambda i: jnp.maximum(i * per - 1, 0)
    next_blk = lambda i: jnp.minimum((i + 1) * per, last)
    v_specs = [row_spec(kvw),
               pl.BlockSpec((None, WINDOW, kvw), lambda b, i: (b, prev_blk(i), 0)),
               pl.BlockSpec((None, WINDOW, kvw), lambda b, i: (b, next_blk(i), 0))]
    in_specs = [
        row_spec(D_MODEL),
        row_spec(POOL_WIDTH),
        row_spec(ATTN_WIDTH),
        row_spec(ATTN_WIDTH),
        row_spec(2 * D_MODEL),
        pl.BlockSpec((None, kvw, tq), lambda b, i: (b, 0, i)),
        pl.BlockSpec((None, kvw, WINDOW), lambda b, i: (b, 0, prev_blk(i))),
        pl.BlockSpec((None, kvw, WINDOW), lambda b, i: (b, 0, next_blk(i))),
        *v_specs, *v_specs,
        const((POOL_WIDTH, D_MODEL)),
        const((N_HEADS, LANES)),
        const((ATTN_WIDTH, D_MODEL)),
        const((D_MODEL, D_MODEL)),
        const((1, D_MODEL)),
    ]
    scratch_shapes = [
        pltpu.VMEM((kvw, tq + 2 * WINDOW), jnp.bfloat16),
        pltpu.VMEM((tq + 2 * WINDOW, kvw), jnp.bfloat16),
        pltpu.VMEM((tq + 2 * WINDOW, kvw), jnp.bfloat16),
        pltpu.VMEM((tq, ATTN_WIDTH), jnp.bfloat16),
    ]
    return pl.pallas_call(
        _mixer_kernel,
        grid=(batch, nt),
        in_specs=in_specs,
        out_specs=row_spec(D_MODEL),
        out_shape=jax.ShapeDtypeStruct((batch, seq, D_MODEL), jnp.float32),
        scratch_shapes=scratch_shapes,
        compiler_params=pltpu.CompilerParams(
            dimension_semantics=("arbitrary", "arbitrary"),
            vmem_limit_bytes=VMEM_LIMIT_BYTES),
        name="mixer",
    )(x, pb, q, ag, gates, kt, kt, kt, vl, vl, vl, vh, vh, vh, wpp, sink, wap, wo, g_post)


def _rope_tables(seq):
    half = HEAD_DIM // 2
    inv = ROPE_THETA ** (-jnp.arange(half, dtype=jnp.float32) / half)
    ang = jnp.arange(seq).astype(jnp.float32)[:, None] * inv[None, :]
    cos = jnp.cos(ang)
    sin = jnp.sin(ang)
    reps = LANES // HEAD_DIM
    cos_t = jnp.tile(jnp.concatenate([cos, cos], axis=-1), (1, reps))
    sin_t = jnp.tile(jnp.concatenate([-sin, sin], axis=-1), (1, reps))
    return cos_t, sin_t


def _encoder_layer(x, norm_pre, w_in, w_pool_group, pool_scale, w_pool_proj,
                   attn_sink, w_attn_proj, w_out, norm_post, *, tm=TM_INPROJ, tq=TQ_MIX):
    batch, seq, _ = x.shape
    tm = min(tm, seq)
    tq = min(tq, seq)
    assert seq % tm == 0 and seq % tq == 0 and tq % BLOCK == 0 and tm % LANES == 0
    bf16 = jnp.bfloat16
    cos_t, sin_t = _rope_tables(seq)
    pb, q, ag, gates, kt, vl, vh = _inproj_call(
        x, norm_pre.reshape(1, D_MODEL), w_in.astype(bf16), cos_t, sin_t,
        w_pool_group.astype(bf16), pool_scale.reshape(1, POOL_WIDTH).astype(jnp.float32), tm=tm)
    sink = jnp.broadcast_to((attn_sink.astype(jnp.float32) * LOG2_E)[:, None], (N_HEADS, LANES))
    return _mixer_call(
        x, pb, q, ag, gates, kt, vl, vh,
        w_pool_proj.astype(bf16), sink, w_attn_proj.astype(bf16), w_out.astype(bf16),
        norm_post.reshape(1, D_MODEL), tq=tq)


def kernel(x_prompt, x_sample, norm_pre, w_in, w_pool_group, pool_scale, w_pool_proj,
           attn_sink, w_attn_proj, w_out, norm_post):
    y_prompt, y_sample = x_prompt, x_sample
    for l in range(norm_pre.shape[0]):
        layer = functools.partial(
            _encoder_layer, norm_pre=norm_pre[l], w_in=w_in[l], w_pool_group=w_pool_group[l],
            pool_scale=pool_scale[l], w_pool_proj=w_pool_proj[l], attn_sink=attn_sink[l],
            w_attn_proj=w_attn_proj[l], w_out=w_out[l], norm_post=norm_post[l])
        y_prompt = layer(y_prompt)
        y_sample = layer(y_sample)
    return (y_prompt, y_sample)
```

```python
import functools
import math

import jax
import jax.numpy as jnp
from jax import lax
from jax.experimental import pallas as pl
from jax.experimental.pallas import tpu as pltpu

D_MODEL = 1024
N_HEADS = 16
N_KV_HEADS = 4
HEAD_DIM = 64
GROUP = N_HEADS // N_KV_HEADS
ATTN_WIDTH = N_HEADS * HEAD_DIM
KV_WIDTH = N_KV_HEADS * HEAD_DIM
WINDOW = 128
BLOCK = 128
SPAN = BLOCK + 2 * WINDOW
ROPE_THETA = 10000.0
POOL_WIDTH = D_MODEL
POOL_WINDOWS = (2, 4, 8, 16)
N_POOL_GROUPS = 4
POOL_GROUP = POOL_WIDTH // N_POOL_GROUPS
RMS_EPS = 1e-6
NEG_INF = -1e30
LOG2_E = math.log2(math.e)

C_U = 0
C_PG = C_U + POOL_WIDTH
C_Q = C_PG + POOL_WIDTH
C_K = C_Q + ATTN_WIDTH
C_V = C_K + KV_WIDTH
C_AG = C_V + KV_WIDTH
C_MG = C_AG + ATTN_WIDTH
IN_WIDTH = C_MG + 2 * D_MODEL

LANES = 128
HALF_LANES = LANES // 2
SUBLANES = 8
POOL_HALO = 8
POOL_PAD = 3 * SUBLANES
KV_SLOT = 2 * LANES
VMEM_LIMIT_BYTES = 56 * 1024 * 1024

TM_INPROJ = 512
TQ_MIX = 512
COL_CHUNK = 512

assert POOL_WINDOWS == (2, 4, 8, 16) and POOL_HALO == max(POOL_WINDOWS) // 2


def _silu(z):
    return z * jax.nn.sigmoid(z)


def _rms_scale(x, g):
    ms = jnp.mean(x * x, axis=-1, keepdims=True)
    return x * lax.rsqrt(ms + RMS_EPS) * g


def _inproj_kernel(x_ref, xp_ref, xn_ref, g_ref, w_ref, cos_ref, sin_ref, wpg_ref, ps_ref,
                   pb_ref, q_ref, ag_ref, gate_ref, kt_ref, vl_ref, vh_ref,
                   uext, *, seq):
    tm = x_ref.shape[0]
    i = pl.program_id(1)
    is_first = i == 0
    is_last = i == pl.num_programs(1) - 1
    g_pre = g_ref[...]
    hf = _rms_scale(x_ref[...], g_pre)
    h = hf.astype(jnp.bfloat16)

    def proj(lo, width):
        return jnp.dot(h, w_ref[:, lo:lo + width], preferred_element_type=jnp.float32)

    lane = lax.broadcasted_iota(jnp.int32, (tm, LANES), 1)
    low_half = lane < HALF_LANES
    rot_first = (lane % HEAD_DIM) < (HEAD_DIM // 2)
    cos = cos_ref[...]
    sin = sin_ref[...]

    def rope(zb):
        partner = jnp.where(rot_first, pltpu.roll(zb, LANES - HEAD_DIM // 2, 1),
                            pltpu.roll(zb, HEAD_DIM // 2, 1))
        return zb * cos + partner * sin

    def split_heads(blk):
        swapped = pltpu.roll(blk, HALF_LANES, 1)
        return ((jnp.where(low_half, blk, 0.0), jnp.where(low_half, 0.0, swapped)),
                (jnp.where(low_half, swapped, 0.0), jnp.where(low_half, 0.0, blk)))

    h_ext = jnp.concatenate(
        [_rms_scale(xp_ref[...], g_pre), hf, _rms_scale(xn_ref[...], g_pre)], axis=0).astype(jnp.bfloat16)
    ext = tm + 2 * POOL_HALO

    def u_task(c):
        def mm():
            return jnp.dot(h_ext, w_ref[:, C_U + c:C_U + c + COL_CHUNK], preferred_element_type=jnp.float32)

        def ep(z):
            keep_top = jnp.logical_not(is_first)
            keep_bottom = jnp.logical_not(is_last)
            uext[0:POOL_HALO, c:c + COL_CHUNK] = jnp.where(keep_top, z[0:POOL_HALO], 0.0)
            uext[POOL_HALO:ext - POOL_HALO, c:c + COL_CHUNK] = z[POOL_HALO:ext - POOL_HALO]
            uext[ext - POOL_HALO:ext, c:c + COL_CHUNK] = jnp.where(keep_bottom, z[ext - POOL_HALO:ext], 0.0)
            uext[ext:ext + POOL_PAD, c:c + COL_CHUNK] = jnp.zeros((POOL_PAD, COL_CHUNK), jnp.float32)
        return mm, ep

    pos = i * tm + lax.broadcasted_iota(jnp.int32, (tm, LANES), 0)
    pool_vals = {}

    def pooled_group(g, w):
        cols = slice(g * POOL_GROUP, (g + 1) * POOL_GROUP)
        u_all = uext[:, cols]
        center = u_all[POOL_HALO:POOL_HALO + tm]
        n1 = tm + 2 * POOL_HALO + SUBLANES
        win2 = u_all[0:n1] + u_all[1:n1 + 1]
        if w == 2:
            total = win2[POOL_HALO - 1:POOL_HALO - 1 + tm]
        elif w == 4:
            total = win2[POOL_HALO - 2:POOL_HALO - 2 + tm] + win2[POOL_HALO:POOL_HALO + tm]
        else:
            n2 = tm + 2 * POOL_HALO
            win4 = win2[0:n2] + win2[2:n2 + 2]
            if w == 8:
                total = win4[POOL_HALO - 4:POOL_HALO - 4 + tm] + win4[POOL_HALO:POOL_HALO + tm]
            else:
                n3 = tm + POOL_HALO
                win8 = win4[0:n3] + win4[4:n3 + 4]
                total = win8[0:tm] + win8[POOL_HALO:POOL_HALO + tm]
        lo = jnp.maximum(pos - w // 2, 0)
        hi = jnp.minimum(pos - w // 2 + w, seq)
        inv_cnt = 1.0 / (hi - lo).astype(jnp.float32)
        inv_cnt = jnp.concatenate([inv_cnt] * (POOL_GROUP // LANES), axis=-1)
        return (total * inv_cnt - center).astype(jnp.bfloat16)

    def pool_gate_task(g, w):
        def ep(z):
            pool_vals[g] = (_silu(z), pooled_group(g, w))
        return (lambda: proj(C_PG + g * POOL_GROUP, POOL_GROUP)), ep

    def pool_mix_task(g):
        cols = slice(g * POOL_GROUP, (g + 1) * POOL_GROUP)

        def mm():
            return jnp.dot(pool_vals[g][1], wpg_ref[g], preferred_element_type=jnp.float32)

        def ep(z):
            pb_ref[:, cols] = (z * ps_ref[:, cols] * pool_vals[g][0]).astype(pb_ref.dtype)
        return mm, ep

    q_scale = (HEAD_DIM ** -0.5) * LOG2_E

    def q_task(c):
        def ep(z):
            for j in range(COL_CHUNK // LANES):
                zb = rope(z[:, j * LANES:(j + 1) * LANES]) * q_scale
                q_ref[:, c + j * LANES:c + (j + 1) * LANES] = zb.astype(q_ref.dtype)
        return (lambda: proj(C_Q + c, COL_CHUNK)), ep

    def act_task(lo, out_ref, c, act):
        def ep(z):
            out_ref[:, c:c + COL_CHUNK] = act(z).astype(out_ref.dtype)
        return (lambda: proj(lo + c, COL_CHUNK)), ep

    def k_ep(zk):
        for j in range(KV_WIDTH // LANES):
            for par, (lowv, highv) in enumerate(split_heads(rope(zk[:, j * LANES:(j + 1) * LANES]))):
                base = (2 * j + par) * KV_SLOT
                kt_ref[base:base + LANES, :] = lowv.T.astype(kt_ref.dtype)
                kt_ref[base + LANES:base + KV_SLOT, :] = highv.T.astype(kt_ref.dtype)

    def v_ep(zv):
        ones_low = jnp.where(low_half, 1.0, 0.0).astype(vl_ref.dtype)
        ones_high = jnp.where(low_half, 0.0, 1.0).astype(vh_ref.dtype)
        for j in range(KV_WIDTH // LANES):
            for par, (lowv, highv) in enumerate(split_heads(zv[:, j * LANES:(j + 1) * LANES])):
                base = (2 * j + par) * KV_SLOT
                vl_ref[:, base:base + LANES] = lowv.astype(vl_ref.dtype)
                vl_ref[:, base + LANES:base + KV_SLOT] = ones_low
                vh_ref[:, base:base + LANES] = highv.astype(vh_ref.dtype)
                vh_ref[:, base + LANES:base + KV_SLOT] = ones_high

    tasks = [u_task(c) for c in range(0, POOL_WIDTH, COL_CHUNK)]
    tasks += [q_task(c) for c in range(0, ATTN_WIDTH, COL_CHUNK)]
    tasks += [pool_gate_task(g, w) for g, w in enumerate(POOL_WINDOWS)]
    tasks += [(lambda: proj(C_K, KV_WIDTH), k_ep), (lambda: proj(C_V, KV_WIDTH), v_ep)]
    tasks += [pool_mix_task(g) for g in range(N_POOL_GROUPS)]
    tasks += [act_task(C_AG, ag_ref, c, _silu) for c in range(0, ATTN_WIDTH, COL_CHUNK)]
    tasks += [act_task(C_MG, gate_ref, c, jax.nn.sigmoid) for c in range(0, 2 * D_MODEL, COL_CHUNK)]
    pending = None
    for mm, ep in tasks:
        z = mm()
        if pending is not None:
            pending[0](pending[1])
        pending = (ep, z)
    pending[0](pending[1])


def _inproj_call(x, g_pre, w_in, cos, sin, wpg, ps, *, tm):
    batch, seq, _ = x.shape
    nst = seq // tm
    bf16 = jnp.bfloat16
    kvw = N_KV_HEADS * KV_SLOT
    row_spec = lambda width: pl.BlockSpec((None, tm, width), lambda b, i: (b, i, 0))
    const = lambda shape: pl.BlockSpec(shape, lambda b, i: (0,) * len(shape))
    per = tm // POOL_HALO
    last = seq // POOL_HALO - 1
    out_shape = (
        jax.ShapeDtypeStruct((batch, seq, POOL_WIDTH), bf16),
        jax.ShapeDtypeStruct((batch, seq, ATTN_WIDTH), bf16),
        jax.ShapeDtypeStruct((batch, seq, ATTN_WIDTH), bf16),
        jax.ShapeDtypeStruct((batch, seq, 2 * D_MODEL), bf16),
        jax.ShapeDtypeStruct((batch, kvw, seq), bf16),
        jax.ShapeDtypeStruct((batch, seq, kvw), bf16),
        jax.ShapeDtypeStruct((batch, seq, kvw), bf16),
    )
    out_specs = (
        row_spec(POOL_WIDTH), row_spec(ATTN_WIDTH), row_spec(ATTN_WIDTH), row_spec(2 * D_MODEL),
        pl.BlockSpec((None, kvw, tm), lambda b, i: (b, 0, i)),
        row_spec(kvw), row_spec(kvw),
    )
    in_specs = [
        row_spec(D_MODEL),
        pl.BlockSpec((None, POOL_HALO, D_MODEL), lambda b, i: (b, jnp.maximum(i * per - 1, 0), 0)),
        pl.BlockSpec((None, POOL_HALO, D_MODEL), lambda b, i: (b, jnp.minimum((i + 1) * per, last), 0)),
        const((1, D_MODEL)),
        const((D_MODEL, IN_WIDTH)),
        pl.BlockSpec((tm, LANES), lambda b, i: (i, 0)),
        pl.BlockSpec((tm, LANES), lambda b, i: (i, 0)),
        const((N_POOL_GROUPS, POOL_GROUP, POOL_GROUP)),
        const((1, POOL_WIDTH)),
    ]
    return pl.pallas_call(
        functools.partial(_inproj_kernel, seq=seq),
        grid=(batch, nst),
        in_specs=in_specs,
        out_specs=out_specs,
        out_shape=out_shape,
        scratch_shapes=[pltpu.VMEM((tm + 2 * POOL_HALO + POOL_PAD, POOL_WIDTH), jnp.float32)],
        compiler_params=pltpu.CompilerParams(
            dimension_semantics=("arbitrary", "arbitrary"),
            vmem_limit_bytes=VMEM_LIMIT_BYTES),
        name="inproj",
    )(x, x, x, g_pre, w_in, cos, sin, wpg, ps)


def _mixer_kernel(x_ref, pb_ref, q_ref, ag_ref, gate_ref,
                  kt_ref, ktp_ref, ktn_ref, vl_ref, vlp_ref, vln_ref, vh_ref, vhp_ref, vhn_ref,
                  wpp_ref, sink_ref, wap_ref, wo_ref, gpost_ref,
                  y_ref,
                  kfull, vlfull, vhfull, ab_ref):
    tq = x_ref.shape[0]
    nqb = tq // BLOCK
    i = pl.program_id(1)
    is_first = i == 0
    is_last = i == pl.num_programs(1) - 1

    kfull[:, 0:WINDOW] = ktp_ref[...]
    kfull[:, WINDOW:WINDOW + tq] = kt_ref[...]
    kfull[:, WINDOW + tq:] = ktn_ref[...]
    for full, prev, cur, nxt in ((vlfull, vlp_ref, vl_ref, vln_ref), (vhfull, vhp_ref, vh_ref, vhn_ref)):
        full[0:WINDOW, :] = prev[...]
        full[WINDOW:WINDOW + tq, :] = cur[...]
        full[WINDOW + tq:, :] = nxt[...]

    qrow = lax.broadcasted_iota(jnp.int32, (BLOCK, LANES), 0)
    kcol = lax.broadcasted_iota(jnp.int32, (BLOCK, LANES), 1)
    low_half = kcol < HALF_LANES

    def chunk_masks(qi):
        first = kcol >= qrow
        last = kcol <= qrow
        if qi == 0:
            first = first & jnp.logical_not(is_first)
        if qi == nqb - 1:
            last = last & jnp.logical_not(is_last)
        return first, last

    masks = [chunk_masks(qi) for qi in range(nqb)]
    n_chunks = SPAN // LANES

    def scores(qi, hkv):
        rows = slice(qi * BLOCK, (qi + 1) * BLOCK)
        keys = slice(qi * BLOCK, qi * BLOCK + SPAN)
        q2 = jnp.concatenate([q_ref[rows, (2 * hkv + r) * LANES:(2 * hkv + r + 1) * LANES]
                              for r in range(GROUP // 2)], axis=0)
        kw = jnp.concatenate([kfull[hkv * KV_SLOT:hkv * KV_SLOT + LANES, keys],
                              kfull[hkv * KV_SLOT + LANES:(hkv + 1) * KV_SLOT, keys]], axis=1)
        return jnp.dot(q2, kw, preferred_element_type=jnp.float32)

    def softmax(qi, hkv, s):
        first_ok, last_ok = masks[qi]
        p_rows, sink_terms = [], []
        for r in range(GROUP // 2):
            p_cols, sink_pair = [], []
            for c in range(2):
                head = hkv * GROUP + 2 * r + c
                blk = s[r * BLOCK:(r + 1) * BLOCK, c * SPAN:(c + 1) * SPAN]
                chunks = [blk[:, k * LANES:(k + 1) * LANES] for k in range(n_chunks)]
                chunks[0] = jnp.where(first_ok, chunks[0], NEG_INF)
                chunks[-1] = jnp.where(last_ok, chunks[-1], NEG_INF)
                top = functools.reduce(jnp.maximum, chunks)
                sink_row = sink_ref[head:head + 1, :]
                m = jnp.maximum(jnp.max(top, axis=-1, keepdims=True), sink_row)
                p_cols += [jnp.exp2(ch - m).astype(jnp.bfloat16) for ch in chunks]
                sink_pair.append(jnp.exp2(sink_row - m))
            p_rows.append(jnp.concatenate(p_cols, axis=-1))
            sink_terms.append(jnp.where(low_half, sink_pair[0], sink_pair[1]))
        return jnp.concatenate(p_rows, axis=0), sink_terms

    def weighted_values(qi, hkv, p):
        keys = slice(qi * BLOCK, qi * BLOCK + SPAN)
        slot = slice(hkv * KV_SLOT, (hkv + 1) * KV_SLOT)
        vw = jnp.concatenate([vlfull[keys, slot], vhfull[keys, slot]], axis=0)
        return jnp.dot(p, vw, preferred_element_type=jnp.float32)

    def finalize(qi, hkv, o, sink_terms):
        rows = slice(qi * BLOCK, (qi + 1) * BLOCK)
        for r in range(GROUP // 2):
            lanes = slice((2 * hkv + r) * LANES, (2 * hkv + r + 1) * LANES)
            values = o[r * BLOCK:(r + 1) * BLOCK, 0:LANES]
            denom = o[r * BLOCK:(r + 1) * BLOCK, LANES:KV_SLOT] + sink_terms[r]
            gated = values / denom * ag_ref[rows, lanes].astype(jnp.float32)
            ab_ref[rows, lanes] = gated.astype(ab_ref.dtype)

    items = [(qi, hkv) for qi in range(nqb) for hkv in range(N_KV_HEADS)]
    s_q, p_q, o_q = {}, {}, {}
    for t in range(len(items) + 3):
        if t < len(items):
            s_q[t] = scores(*items[t])
        if 0 <= t - 1 < len(items):
            p_q[t - 1] = softmax(*items[t - 1], s_q.pop(t - 1))
        if 0 <= t - 2 < len(items):
            p, sink_terms = p_q.pop(t - 2)
            o_q[t - 2] = (weighted_values(*items[t - 2], p), sink_terms)
        if 0 <= t - 3 < len(items):
            finalize(*items[t - 3], *o_q.pop(t - 3))

    pp = jnp.dot(pb_ref[...], wpp_ref[...], preferred_element_type=jnp.float32)
    ap = jnp.dot(ab_ref[...], wap_ref[...], preferred_element_type=jnp.float32)
    merged = (gate_ref[:, :D_MODEL].astype(jnp.float32) * pp
              + gate_ref[:, D_MODEL:].astype(jnp.float32) * ap)
    out = jnp.dot(merged.astype(jnp.bfloat16), wo_ref[...], preferred_element_type=jnp.float32)
    y_ref[...] = x_ref[...] + _rms_scale(out, gpost_ref[...])


def _mixer_call(x, pb, q, ag, gates, kt, vl, vh, wpp, sink, wap, wo, g_post, *, tq):
    batch, seq, _ = x.shape
    nt = seq // tq
    kvw = N_KV_HEADS * KV_SLOT
    row_spec = lambda width: pl.BlockSpec((None, tq, width), lambda b, i: (b, i, 0))
    const = lambda shape: pl.BlockSpec(shape, lambda b, i: (0,) * len(shape))
    per = tq // WINDOW
    last = seq // WINDOW - 1
    prev_blk = lambda i: jnp.maximum(i * per - 1, 0)
    next_blk = lambda i: jnp.minimum((i + 1) * per, last)
    v_specs = [row_spec(kvw),
               pl.BlockSpec((None, WINDOW, kvw), lambda b, i: (b, prev_blk(i), 0)),
               pl.BlockSpec((None, WINDOW, kvw), lambda b, i: (b, next_blk(i), 0))]
    in_specs = [
        row_spec(D_MODEL),
        row_spec(POOL_WIDTH),
        row_spec(ATTN_WIDTH),
        row_spec(ATTN_WIDTH),
        row_spec(2 * D_MODEL),
        pl.BlockSpec((None, kvw, tq), lambda b, i: (b, 0, i)),
        pl.BlockSpec((None, kvw, WINDOW), lambda b, i: (b, 0, prev_blk(i))),
        pl.BlockSpec((None, kvw, WINDOW), lambda b, i: (b, 0, next_blk(i))),
        *v_specs, *v_specs,
        const((POOL_WIDTH, D_MODEL)),
        const((N_HEADS, LANES)),
        const((ATTN_WIDTH, D_MODEL)),
        const((D_MODEL, D_MODEL)),
        const((1, D_MODEL)),
    ]
    scratch_shapes = [
        pltpu.VMEM((kvw, tq + 2 * WINDOW), jnp.bfloat16),
        pltpu.VMEM((tq + 2 * WINDOW, kvw), jnp.bfloat16),
        pltpu.VMEM((tq + 2 * WINDOW, kvw), jnp.bfloat16),
        pltpu.VMEM((tq, ATTN_WIDTH), jnp.bfloat16),
    ]
    return pl.pallas_call(
        _mixer_kernel,
        grid=(batch, nt),
        in_specs=in_specs,
        out_specs=row_spec(D_MODEL),
        out_shape=jax.ShapeDtypeStruct((batch, seq, D_MODEL), jnp.float32),
        scratch_shapes=scratch_shapes,
        compiler_params=pltpu.CompilerParams(
            dimension_semantics=("arbitrary", "arbitrary"),
            vmem_limit_bytes=VMEM_LIMIT_BYTES),
        name="mixer",
    )(x, pb, q, ag, gates, kt, kt, kt, vl, vl, vl, vh, vh, vh, wpp, sink, wap, wo, g_post)


def _rope_tables(seq):
    half = HEAD_DIM // 2
    inv = ROPE_THETA ** (-jnp.arange(half, dtype=jnp.float32) / half)
    ang = jnp.arange(seq).astype(jnp.float32)[:, None] * inv[None, :]
    cos = jnp.cos(ang)
    sin = jnp.sin(ang)
    reps = LANES // HEAD_DIM
    cos_t = jnp.tile(jnp.concatenate([cos, cos], axis=-1), (1, reps))
    sin_t = jnp.tile(jnp.concatenate([-sin, sin], axis=-1), (1, reps))
    return cos_t, sin_t


def _encoder_layer(x, norm_pre, w_in, w_pool_group, pool_scale, w_pool_proj,
                   attn_sink, w_attn_proj, w_out, norm_post, *, tm=TM_INPROJ, tq=TQ_MIX):
    batch, seq, _ = x.shape
    tm = min(tm, seq)
    tq = min(tq, seq)
    assert seq % tm == 0 and seq % tq == 0 and tq % BLOCK == 0 and tm % LANES == 0
    bf16 = jnp.bfloat16
    cos_t, sin_t = _rope_tables(seq)
    pb, q, ag, gates, kt, vl, vh = _inproj_call(
        x, norm_pre.reshape(1, D_MODEL), w_in.astype(bf16), cos_t, sin_t,
        w_pool_group.astype(bf16), pool_scale.reshape(1, POOL_WIDTH).astype(jnp.float32), tm=tm)
    sink = jnp.broadcast_to((attn_sink.astype(jnp.float32) * LOG2_E)[:, None], (N_HEADS, LANES))
    return _mixer_call(
        x, pb, q, ag, gates, kt, vl, vh,
        w_pool_proj.astype(bf16), sink, w_attn_proj.astype(bf16), w_out.astype(bf16),
        norm_post.reshape(1, D_MODEL), tq=tq)


def kernel(x_prompt, x_sample, norm_pre, w_in, w_pool_group, pool_scale, w_pool_proj,
           attn_sink, w_attn_proj, w_out, norm_post):
    y_prompt, y_sample = x_prompt, x_sample
    for l in range(norm_pre.shape[0]):
        layer = functools.partial(
            _encoder_layer, norm_pre=norm_pre[l], w_in=w_in[l], w_pool_group=w_pool_group[l],
            pool_scale=pool_scale[l], w_pool_proj=w_pool_proj[l], attn_sink=attn_sink[l],
            w_attn_proj=w_attn_proj[l], w_out=w_out[l], norm_post=norm_post[l])
        y_prompt = layer(y_prompt)
        y_sample = layer(y_sample)
    return (y_prompt, y_sample)
```

```python
import functools
import math

import jax
import jax.numpy as jnp
from jax import lax
from jax.experimental import pallas as pl
from jax.experimental.pallas import tpu as pltpu

D_MODEL = 1024
N_HEADS = 16
N_KV_HEADS = 4
HEAD_DIM = 64
GROUP = N_HEADS // N_KV_HEADS
ATTN_WIDTH = N_HEADS * HEAD_DIM
KV_WIDTH = N_KV_HEADS * HEAD_DIM
WINDOW = 128
BLOCK = 128
SPAN = BLOCK + 2 * WINDOW
ROPE_THETA = 10000.0
POOL_WIDTH = D_MODEL
POOL_WINDOWS = (2, 4, 8, 16)
N_POOL_GROUPS = 4
POOL_GROUP = POOL_WIDTH // N_POOL_GROUPS
RMS_EPS = 1e-6
NEG_INF = -1e30
LOG2_E = math.log2(math.e)

C_U = 0
C_PG = C_U + POOL_WIDTH
C_Q = C_PG + POOL_WIDTH
C_K = C_Q + ATTN_WIDTH
C_V = C_K + KV_WIDTH
C_AG = C_V + KV_WIDTH
C_MG = C_AG + ATTN_WIDTH
IN_WIDTH = C_MG + 2 * D_MODEL

LANES = 128
HALF_LANES = LANES // 2
SUBLANES = 8
POOL_HALO = 8
POOL_PAD = 3 * SUBLANES
KV_SLOT = 2 * LANES
VMEM_LIMIT_BYTES = 56 * 1024 * 1024

TM_INPROJ = 512
TQ_MIX = 512
COL_CHUNK = 512
FILL_CHUNK = 256
LAG_SOFTMAX = 1
LAG_VALUES = 2
LAG_FINALIZE = 3

assert POOL_WINDOWS == (2, 4, 8, 16) and POOL_HALO == max(POOL_WINDOWS) // 2


def _silu(z):
    return z * jax.nn.sigmoid(z)


def _rms_scale(x, g):
    ms = jnp.mean(x * x, axis=-1, keepdims=True)
    return x * lax.rsqrt(ms + RMS_EPS) * g


def _inproj_kernel(x_ref, xp_ref, xn_ref, g_ref, w_ref, cos_ref, sin_ref, wpg_ref, ps_ref,
                   pb_ref, q_ref, ag_ref, gate_ref, kt_ref, vl_ref, vh_ref,
                   uext, *, seq):
    tm = x_ref.shape[0]
    i = pl.program_id(1)
    is_first = i == 0
    is_last = i == pl.num_programs(1) - 1
    g_pre = g_ref[...]
    hf = _rms_scale(x_ref[...], g_pre)
    h = hf.astype(jnp.bfloat16)

    def proj(lo, width):
        return jnp.dot(h, w_ref[:, lo:lo + width], preferred_element_type=jnp.float32)

    lane = lax.broadcasted_iota(jnp.int32, (tm, LANES), 1)
    low_half = lane < HALF_LANES
    rot_first = (lane % HEAD_DIM) < (HEAD_DIM // 2)
    cos = cos_ref[...]
    sin = sin_ref[...]

    def rope(zb):
        partner = jnp.where(rot_first, pltpu.roll(zb, LANES - HEAD_DIM // 2, 1),
                            pltpu.roll(zb, HEAD_DIM // 2, 1))
        return zb * cos + partner * sin

    def split_heads(blk):
        swapped = pltpu.roll(blk, HALF_LANES, 1)
        return ((jnp.where(low_half, blk, 0.0), jnp.where(low_half, 0.0, swapped)),
                (jnp.where(low_half, swapped, 0.0), jnp.where(low_half, 0.0, blk)))

    h_ext = jnp.concatenate(
        [_rms_scale(xp_ref[...], g_pre), hf, _rms_scale(xn_ref[...], g_pre)], axis=0).astype(jnp.bfloat16)
    ext = tm + 2 * POOL_HALO

    def u_task(c):
        def mm():
            return jnp.dot(h_ext, w_ref[:, C_U + c:C_U + c + COL_CHUNK], preferred_element_type=jnp.float32)

        def ep(z):
            keep_top = jnp.logical_not(is_first)
            keep_bottom = jnp.logical_not(is_last)
            uext[0:POOL_HALO, c:c + COL_CHUNK] = jnp.where(keep_top, z[0:POOL_HALO], 0.0)
            uext[POOL_HALO:ext - POOL_HALO, c:c + COL_CHUNK] = z[POOL_HALO:ext - POOL_HALO]
            uext[ext - POOL_HALO:ext, c:c + COL_CHUNK] = jnp.where(keep_bottom, z[ext - POOL_HALO:ext], 0.0)
            uext[ext:ext + POOL_PAD, c:c + COL_CHUNK] = jnp.zeros((POOL_PAD, COL_CHUNK), jnp.float32)
        return mm, ep

    pos = i * tm + lax.broadcasted_iota(jnp.int32, (tm, LANES), 0)
    pool_vals = {}

    def pooled_group(g, w):
        cols = slice(g * POOL_GROUP, (g + 1) * POOL_GROUP)
        u_all = uext[:, cols]
        center = u_all[POOL_HALO:POOL_HALO + tm]
        n1 = tm + 2 * POOL_HALO + SUBLANES
        win2 = u_all[0:n1] + u_all[1:n1 + 1]
        if w == 2:
            total = win2[POOL_HALO - 1:POOL_HALO - 1 + tm]
        elif w == 4:
            total = win2[POOL_HALO - 2:POOL_HALO - 2 + tm] + win2[POOL_HALO:POOL_HALO + tm]
        else:
            n2 = tm + 2 * POOL_HALO
            win4 = win2[0:n2] + win2[2:n2 + 2]
            if w == 8:
                total = win4[POOL_HALO - 4:POOL_HALO - 4 + tm] + win4[POOL_HALO:POOL_HALO + tm]
            else:
                n3 = tm + POOL_HALO
                win8 = win4[0:n3] + win4[4:n3 + 4]
                total = win8[0:tm] + win8[POOL_HALO:POOL_HALO + tm]
        lo = jnp.maximum(pos - w // 2, 0)
        hi = jnp.minimum(pos - w // 2 + w, seq)
        inv_cnt = 1.0 / (hi - lo).astype(jnp.float32)
        inv_cnt = jnp.concatenate([inv_cnt] * (POOL_GROUP // LANES), axis=-1)
        return (total * inv_cnt - center).astype(jnp.bfloat16)

    def pool_gate_task(g, w):
        def ep(z):
            pool_vals[g] = (_silu(z), pooled_group(g, w))
        return (lambda: proj(C_PG + g * POOL_GROUP, POOL_GROUP)), ep

    def pool_mix_task(g):
        cols = slice(g * POOL_GROUP, (g + 1) * POOL_GROUP)

        def mm():
            return jnp.dot(pool_vals[g][1], wpg_ref[g], preferred_element_type=jnp.float32)

        def ep(z):
            pb_ref[:, cols] = (z * ps_ref[:, cols] * pool_vals[g][0]).astype(pb_ref.dtype)
        return mm, ep

    q_scale = (HEAD_DIM ** -0.5) * LOG2_E

    def q_task(c):
        def ep(z):
            for j in range(COL_CHUNK // LANES):
                zb = rope(z[:, j * LANES:(j + 1) * LANES]) * q_scale
                q_ref[:, c + j * LANES:c + (j + 1) * LANES] = zb.astype(q_ref.dtype)
        return (lambda: proj(C_Q + c, COL_CHUNK)), ep

    def act_task(lo, out_ref, c, act):
        def ep(z):
            out_ref[:, c:c + COL_CHUNK] = act(z).astype(out_ref.dtype)
        return (lambda: proj(lo + c, COL_CHUNK)), ep

    def k_ep(zk):
        for j in range(KV_WIDTH // LANES):
            for par, (lowv, highv) in enumerate(split_heads(rope(zk[:, j * LANES:(j + 1) * LANES]))):
                base = (2 * j + par) * KV_SLOT
                kt_ref[base:base + LANES, :] = lowv.T.astype(kt_ref.dtype)
                kt_ref[base + LANES:base + KV_SLOT, :] = highv.T.astype(kt_ref.dtype)

    def v_ep(zv):
        ones_low = jnp.where(low_half, 1.0, 0.0).astype(vl_ref.dtype)
        ones_high = jnp.where(low_half, 0.0, 1.0).astype(vh_ref.dtype)
        for j in range(KV_WIDTH // LANES):
            for par, (lowv, highv) in enumerate(split_heads(zv[:, j * LANES:(j + 1) * LANES])):
                base = (2 * j + par) * KV_SLOT
                vl_ref[:, base:base + LANES] = lowv.astype(vl_ref.dtype)
                vl_ref[:, base + LANES:base + KV_SLOT] = ones_low
                vh_ref[:, base:base + LANES] = highv.astype(vh_ref.dtype)
                vh_ref[:, base + LANES:base + KV_SLOT] = ones_high

    tasks = [u_task(c) for c in range(0, POOL_WIDTH, COL_CHUNK)]
    tasks += [q_task(c) for c in range(0, ATTN_WIDTH, COL_CHUNK)]
    tasks += [pool_gate_task(g, w) for g, w in enumerate(POOL_WINDOWS)]
    tasks += [(lambda: proj(C_K, KV_WIDTH), k_ep), (lambda: proj(C_V, KV_WIDTH), v_ep)]
    tasks += [pool_mix_task(g) for g in range(N_POOL_GROUPS)]
    tasks += [act_task(C_AG, ag_ref, c, _silu) for c in range(0, ATTN_WIDTH, COL_CHUNK)]
    tasks += [act_task(C_MG, gate_ref, c, jax.nn.sigmoid) for c in range(0, 2 * D_MODEL, COL_CHUNK)]
    pending = None
    for mm, ep in tasks:
        z = mm()
        if pending is not None:
            pending[0](pending[1])
        pending = (ep, z)
    pending[0](pending[1])


def _inproj_call(x, g_pre, w_in, cos, sin, wpg, ps, *, tm):
    batch, seq, _ = x.shape
    nst = seq // tm
    bf16 = jnp.bfloat16
    kvw = N_KV_HEADS * KV_SLOT
    row_spec = lambda width: pl.BlockSpec((None, tm, width), lambda b, i: (b, i, 0))
    const = lambda shape: pl.BlockSpec(shape, lambda b, i: (0,) * len(shape))
    per = tm // POOL_HALO
    last = seq // POOL_HALO - 1
    out_shape = (
        jax.ShapeDtypeStruct((batch, seq, POOL_WIDTH), bf16),
        jax.ShapeDtypeStruct((batch, seq, ATTN_WIDTH), bf16),
        jax.ShapeDtypeStruct((batch, seq, ATTN_WIDTH), bf16),
        jax.ShapeDtypeStruct((batch, seq, 2 * D_MODEL), bf16),
        jax.ShapeDtypeStruct((batch, kvw, seq), bf16),
        jax.ShapeDtypeStruct((batch, seq, kvw), bf16),
        jax.ShapeDtypeStruct((batch, seq, kvw), bf16),
    )
    out_specs = (
        row_spec(POOL_WIDTH), row_spec(ATTN_WIDTH), row_spec(ATTN_WIDTH), row_spec(2 * D_MODEL),
        pl.BlockSpec((None, kvw, tm), lambda b, i: (b, 0, i)),
        row_spec(kvw), row_spec(kvw),
    )
    in_specs = [
        row_spec(D_MODEL),
        pl.BlockSpec((None, POOL_HALO, D_MODEL), lambda b, i: (b, jnp.maximum(i * per - 1, 0), 0)),
        pl.BlockSpec((None, POOL_HALO, D_MODEL), lambda b, i: (b, jnp.minimum((i + 1) * per, last), 0)),
        const((1, D_MODEL)),
        const((D_MODEL, IN_WIDTH)),
        pl.BlockSpec((tm, LANES), lambda b, i: (i, 0)),
        pl.BlockSpec((tm, LANES), lambda b, i: (i, 0)),
        const((N_POOL_GROUPS, POOL_GROUP, POOL_GROUP)),
        const((1, POOL_WIDTH)),
    ]
    return pl.pallas_call(
        functools.partial(_inproj_kernel, seq=seq),
        grid=(batch, nst),
        in_specs=in_specs,
        out_specs=out_specs,
        out_shape=out_shape,
        scratch_shapes=[pltpu.VMEM((tm + 2 * POOL_HALO + POOL_PAD, POOL_WIDTH), jnp.float32)],
        compiler_params=pltpu.CompilerParams(
            dimension_semantics=("arbitrary", "arbitrary"),
            vmem_limit_bytes=VMEM_LIMIT_BYTES),
        name="inproj",
    )(x, x, x, g_pre, w_in, cos, sin, wpg, ps)


def _mixer_kernel(x_ref, pb_ref, q_ref, ag_ref, gate_ref,
                  kt_ref, ktp_ref, ktn_ref, vl_ref, vlp_ref, vln_ref, vh_ref, vhp_ref, vhn_ref,
                  wpp_ref, sink_ref, wap_ref, wo_ref, gpost_ref,
                  y_ref,
                  kfull, vlfull, vhfull, ab_ref, pm_ref):
    tq = x_ref.shape[0]
    nqb = tq // BLOCK
    i = pl.program_id(1)
    is_first = i == 0
    is_last = i == pl.num_programs(1) - 1

    kfull[:, 0:WINDOW] = ktp_ref[...]
    kfull[:, WINDOW:WINDOW + tq] = kt_ref[...]
    kfull[:, WINDOW + tq:] = ktn_ref[...]
    for full, prev, cur, nxt in ((vlfull, vlp_ref, vl_ref, vln_ref), (vhfull, vhp_ref, vh_ref, vhn_ref)):
        full[0:WINDOW, :] = prev[...]
        full[WINDOW:WINDOW + tq, :] = cur[...]
        full[WINDOW + tq:, :] = nxt[...]

    qrow = lax.broadcasted_iota(jnp.int32, (BLOCK, LANES), 0)
    kcol = lax.broadcasted_iota(jnp.int32, (BLOCK, LANES), 1)
    low_half = kcol < HALF_LANES

    def chunk_masks(qi):
        first = kcol >= qrow
        last = kcol <= qrow
        if qi == 0:
            first = first & jnp.logical_not(is_first)
        if qi == nqb - 1:
            last = last & jnp.logical_not(is_last)
        return first, last

    masks = [chunk_masks(qi) for qi in range(nqb)]
    n_chunks = SPAN // LANES

    def scores(qi, hkv):
        rows = slice(qi * BLOCK, (qi + 1) * BLOCK)
        keys = slice(qi * BLOCK, qi * BLOCK + SPAN)
        q2 = jnp.concatenate([q_ref[rows, (2 * hkv + r) * LANES:(2 * hkv + r + 1) * LANES]
                              for r in range(GROUP // 2)], axis=0)
        kw = jnp.concatenate([kfull[hkv * KV_SLOT:hkv * KV_SLOT + LANES, keys],
                              kfull[hkv * KV_SLOT + LANES:(hkv + 1) * KV_SLOT, keys]], axis=1)
        return jnp.dot(q2, kw, preferred_element_type=jnp.float32)

    def softmax(qi, hkv, s):
        first_ok, last_ok = masks[qi]
        p_rows, sink_terms = [], []
        for r in range(GROUP // 2):
            p_cols, sink_pair = [], []
            for c in range(2):
                head = hkv * GROUP + 2 * r + c
                blk = s[r * BLOCK:(r + 1) * BLOCK, c * SPAN:(c + 1) * SPAN]
                chunks = [blk[:, k * LANES:(k + 1) * LANES] for k in range(n_chunks)]
                chunks[0] = jnp.where(first_ok, chunks[0], NEG_INF)
                chunks[-1] = jnp.where(last_ok, chunks[-1], NEG_INF)
                top = functools.reduce(jnp.maximum, chunks)
                sink_row = sink_ref[head:head + 1, :]
                m = jnp.maximum(jnp.max(top, axis=-1, keepdims=True), sink_row)
                p_cols += [jnp.exp2(ch - m).astype(jnp.bfloat16) for ch in chunks]
                sink_pair.append(jnp.exp2(sink_row - m))
            p_rows.append(jnp.concatenate(p_cols, axis=-1))
            sink_terms.append(jnp.where(low_half, sink_pair[0], sink_pair[1]))
        return jnp.concatenate(p_rows, axis=0), sink_terms

    def weighted_values(qi, hkv, p):
        keys = slice(qi * BLOCK, qi * BLOCK + SPAN)
        slot = slice(hkv * KV_SLOT, (hkv + 1) * KV_SLOT)
        vw = jnp.concatenate([vlfull[keys, slot], vhfull[keys, slot]], axis=0)
        return jnp.dot(p, vw, preferred_element_type=jnp.float32)

    def finalize(qi, hkv, o, sink_terms):
        rows = slice(qi * BLOCK, (qi + 1) * BLOCK)
        for r in range(GROUP // 2):
            lanes = slice((2 * hkv + r) * LANES, (2 * hkv + r + 1) * LANES)
            values = o[r * BLOCK:(r + 1) * BLOCK, 0:LANES]
            denom = o[r * BLOCK:(r + 1) * BLOCK, LANES:KV_SLOT] + sink_terms[r]
            gated = values / denom * ag_ref[rows, lanes].astype(jnp.float32)
            ab_ref[rows, lanes] = gated.astype(ab_ref.dtype)

    items = [(qi, hkv) for qi in range(nqb) for hkv in range(N_KV_HEADS)]

    n_fill = D_MODEL // FILL_CHUNK
    n_steps = len(items) + LAG_FINALIZE
    fill_at = {(k * len(items)) // n_fill + 1: k for k in range(n_fill)}

    def pool_fill(k):
        cols = slice(k * FILL_CHUNK, (k + 1) * FILL_CHUNK)
        pp = jnp.dot(pb_ref[...], wpp_ref[:, cols], preferred_element_type=jnp.float32)
        pm_ref[:, cols] = gate_ref[:, cols].astype(jnp.float32) * pp

    s_q, p_q, o_q = {}, {}, {}
    for t in range(n_steps):
        if t < len(items):
            s_q[t] = scores(*items[t])
        if t in fill_at:
            pool_fill(fill_at[t])
        a, b, c = t - LAG_SOFTMAX, t - LAG_VALUES, t - LAG_FINALIZE
        if 0 <= a < len(items):
            p_q[a] = softmax(*items[a], s_q.pop(a))
        if 0 <= b < len(items):
            p, sink_terms = p_q.pop(b)
            o_q[b] = (weighted_values(*items[b], p), sink_terms)
        if 0 <= c < len(items):
            finalize(*items[c], *o_q.pop(c))

    ap = jnp.dot(ab_ref[...], wap_ref[...], preferred_element_type=jnp.float32)
    merged = pm_ref[...] + gate_ref[:, D_MODEL:].astype(jnp.float32) * ap
    out = jnp.dot(merged.astype(jnp.bfloat16), wo_ref[...], preferred_element_type=jnp.float32)
    y_ref[...] = x_ref[...] + _rms_scale(out, gpost_ref[...])


def _mixer_call(x, pb, q, ag, gates, kt, vl, vh, wpp, sink, wap, wo, g_post, *, tq):
    batch, seq, _ = x.shape
    nt = seq // tq
    kvw = N_KV_HEADS * KV_SLOT
    row_spec = lambda width: pl.BlockSpec((None, tq, width), lambda b, i: (b, i, 0))
    const = lambda shape: pl.BlockSpec(shape, lambda b, i: (0,) * len(shape))
    per = tq // WINDOW
    last = seq // WINDOW - 1
    prev_blk = lambda i: jnp.maximum(i * per - 1, 0)
    next_blk = lambda i: jnp.minimum((i + 1) * per, last)
    v_specs = [row_spec(kvw),
               pl.BlockSpec((None, WINDOW, kvw), lambda b, i: (b, prev_blk(i), 0)),
               pl.BlockSpec((None, WINDOW, kvw), lambda b, i: (b, next_blk(i), 0))]
    in_specs = [
        row_spec(D_MODEL),
        row_spec(POOL_WIDTH),
        row_spec(ATTN_WIDTH),
        row_spec(ATTN_WIDTH),
        row_spec(2 * D_MODEL),
        pl.BlockSpec((None, kvw, tq), lambda b, i: (b, 0, i)),
        pl.BlockSpec((None, kvw, WINDOW), lambda b, i: (b, 0, prev_blk(i))),
        pl.BlockSpec((None, kvw, WINDOW), lambda b, i: (b, 0, next_blk(i))),
        *v_specs, *v_specs,
        const((POOL_WIDTH, D_MODEL)),
        const((N_HEADS, LANES)),
        const((ATTN_WIDTH, D_MODEL)),
        const((D_MODEL, D_MODEL)),
        const((1, D_MODEL)),
    ]
    scratch_shapes = [
        pltpu.VMEM((kvw, tq + 2 * WINDOW), jnp.bfloat16),
        pltpu.VMEM((tq + 2 * WINDOW, kvw), jnp.bfloat16),
        pltpu.VMEM((tq + 2 * WINDOW, kvw), jnp.bfloat16),
        pltpu.VMEM((tq, ATTN_WIDTH), jnp.bfloat16),
        pltpu.VMEM((tq, D_MODEL), jnp.float32),
    ]
    return pl.pallas_call(
        _mixer_kernel,
        grid=(batch, nt),
        in_specs=in_specs,
        out_specs=row_spec(D_MODEL),
        out_shape=jax.ShapeDtypeStruct((batch, seq, D_MODEL), jnp.float32),
        scratch_shapes=scratch_shapes,
        compiler_params=pltpu.CompilerParams(
            dimension_semantics=("arbitrary", "arbitrary"),
            vmem_limit_bytes=VMEM_LIMIT_BYTES),
        name="mixer",
    )(x, pb, q, ag, gates, kt, kt, kt, vl, vl, vl, vh, vh, vh, wpp, sink, wap, wo, g_post)


def _rope_tables(seq):
    half = HEAD_DIM // 2
    inv = ROPE_THETA ** (-jnp.arange(half, dtype=jnp.float32) / half)
    ang = jnp.arange(seq).astype(jnp.float32)[:, None] * inv[None, :]
    cos = jnp.cos(ang)
    sin = jnp.sin(ang)
    reps = LANES // HEAD_DIM
    cos_t = jnp.tile(jnp.concatenate([cos, cos], axis=-1), (1, reps))
    sin_t = jnp.tile(jnp.concatenate([-sin, sin], axis=-1), (1, reps))
    return cos_t, sin_t


def _encoder_layer(x, norm_pre, w_in, w_pool_group, pool_scale, w_pool_proj,
                   attn_sink, w_attn_proj, w_out, norm_post, *, tm=TM_INPROJ, tq=TQ_MIX):
    batch, seq, _ = x.shape
    tm = min(tm, seq)
    tq = min(tq, seq)
    assert seq % tm == 0 and seq % tq == 0 and tq % BLOCK == 0 and tm % LANES == 0
    bf16 = jnp.bfloat16
    cos_t, sin_t = _rope_tables(seq)
    pb, q, ag, gates, kt, vl, vh = _inproj_call(
        x, norm_pre.reshape(1, D_MODEL), w_in.astype(bf16), cos_t, sin_t,
        w_pool_group.astype(bf16), pool_scale.reshape(1, POOL_WIDTH).astype(jnp.float32), tm=tm)
    sink = jnp.broadcast_to((attn_sink.astype(jnp.float32) * LOG2_E)[:, None], (N_HEADS, LANES))
    return _mixer_call(
        x, pb, q, ag, gates, kt, vl, vh,
        w_pool_proj.astype(bf16), sink, w_attn_proj.astype(bf16), w_out.astype(bf16),
        norm_post.reshape(1, D_MODEL), tq=tq)


def kernel(x_prompt, x_sample, norm_pre, w_in, w_pool_group, pool_scale, w_pool_proj,
           attn_sink, w_attn_proj, w_out, norm_post):
    y_prompt, y_sample = x_prompt, x_sample
    for l in range(norm_pre.shape[0]):
        layer = functools.partial(
            _encoder_layer, norm_pre=norm_pre[l], w_in=w_in[l], w_pool_group=w_pool_group[l],
            pool_scale=pool_scale[l], w_pool_proj=w_pool_proj[l], attn_sink=attn_sink[l],
            w_attn_proj=w_attn_proj[l], w_out=w_out[l], norm_post=norm_post[l])
        y_prompt = layer(y_prompt)
        y_sample = layer(y_sample)
    return (y_prompt, y_sample)
```

```python
import functools
import math

import jax
import jax.numpy as jnp
from jax import lax
from jax.experimental import pallas as pl
from jax.experimental.pallas import tpu as pltpu

D_MODEL = 1024
N_HEADS = 16
N_KV_HEADS = 4
HEAD_DIM = 64
GROUP = N_HEADS // N_KV_HEADS
ATTN_WIDTH = N_HEADS * HEAD_DIM
KV_WIDTH = N_KV_HEADS * HEAD_DIM
WINDOW = 128
BLOCK = 128
SPAN = BLOCK + 2 * WINDOW
ROPE_THETA = 10000.0
POOL_WIDTH = D_MODEL
POOL_WINDOWS = (2, 4, 8, 16)
N_POOL_GROUPS = 4
POOL_GROUP = POOL_WIDTH // N_POOL_GROUPS
RMS_EPS = 1e-6
NEG_INF = -1e30
LOG2_E = math.log2(math.e)

C_U = 0
C_PG = C_U + POOL_WIDTH
C_Q = C_PG + POOL_WIDTH
C_K = C_Q + ATTN_WIDTH
C_V = C_K + KV_WIDTH
C_AG = C_V + KV_WIDTH
C_MG = C_AG + ATTN_WIDTH
IN_WIDTH = C_MG + 2 * D_MODEL

LANES = 128
HALF_LANES = LANES // 2
SUBLANES = 8
POOL_HALO = 8
POOL_PAD = 3 * SUBLANES
KV_SLOT = 2 * LANES
VMEM_LIMIT_BYTES = 56 * 1024 * 1024

TM_INPROJ = 512
TQ_MIX = 512
COL_CHUNK = 512
FILL_CHUNK = 256
LAG_SOFTMAX = 1
LAG_VALUES = 2
LAG_FINALIZE = 3

assert POOL_WINDOWS == (2, 4, 8, 16) and POOL_HALO == max(POOL_WINDOWS) // 2


def _silu(z):
    return z * jax.nn.sigmoid(z)


def _rms_scale(x, g):
    ms = jnp.mean(x * x, axis=-1, keepdims=True)
    return x * lax.rsqrt(ms + RMS_EPS) * g


def _inproj_kernel(x_ref, xp_ref, xn_ref, g_ref, w_ref, cos_ref, sin_ref, wpg_ref, ps_ref,
                   pb_ref, q_ref, ag_ref, gate_ref, kt_ref, v_ref,
                   uext, *, seq):
    tm = x_ref.shape[0]
    i = pl.program_id(1)
    is_first = i == 0
    is_last = i == pl.num_programs(1) - 1
    g_pre = g_ref[...]
    hf = _rms_scale(x_ref[...], g_pre)
    h = hf.astype(jnp.bfloat16)

    def proj(lo, width):
        return jnp.dot(h, w_ref[:, lo:lo + width], preferred_element_type=jnp.float32)

    lane = lax.broadcasted_iota(jnp.int32, (tm, LANES), 1)
    rot_first = (lane % HEAD_DIM) < (HEAD_DIM // 2)
    cos = cos_ref[...]
    sin = sin_ref[...]

    def rope(zb):
        partner = jnp.where(rot_first, pltpu.roll(zb, LANES - HEAD_DIM // 2, 1),
                            pltpu.roll(zb, HEAD_DIM // 2, 1))
        return zb * cos + partner * sin

    h_ext = jnp.concatenate(
        [_rms_scale(xp_ref[...], g_pre), hf, _rms_scale(xn_ref[...], g_pre)], axis=0).astype(jnp.bfloat16)
    ext = tm + 2 * POOL_HALO

    def u_task(c):
        def mm():
            return jnp.dot(h_ext, w_ref[:, C_U + c:C_U + c + COL_CHUNK], preferred_element_type=jnp.float32)

        def ep(z):
            keep_top = jnp.logical_not(is_first)
            keep_bottom = jnp.logical_not(is_last)
            uext[0:POOL_HALO, c:c + COL_CHUNK] = jnp.where(keep_top, z[0:POOL_HALO], 0.0)
            uext[POOL_HALO:ext - POOL_HALO, c:c + COL_CHUNK] = z[POOL_HALO:ext - POOL_HALO]
            uext[ext - POOL_HALO:ext, c:c + COL_CHUNK] = jnp.where(keep_bottom, z[ext - POOL_HALO:ext], 0.0)
            uext[ext:ext + POOL_PAD, c:c + COL_CHUNK] = jnp.zeros((POOL_PAD, COL_CHUNK), jnp.float32)
        return mm, ep

    pos = i * tm + lax.broadcasted_iota(jnp.int32, (tm, LANES), 0)
    pool_vals = {}

    def pooled_group(g, w):
        cols = slice(g * POOL_GROUP, (g + 1) * POOL_GROUP)
        u_all = uext[:, cols]
        center = u_all[POOL_HALO:POOL_HALO + tm]
        n1 = tm + 2 * POOL_HALO + SUBLANES
        win2 = u_all[0:n1] + u_all[1:n1 + 1]
        if w == 2:
            total = win2[POOL_HALO - 1:POOL_HALO - 1 + tm]
        elif w == 4:
            total = win2[POOL_HALO - 2:POOL_HALO - 2 + tm] + win2[POOL_HALO:POOL_HALO + tm]
        else:
            n2 = tm + 2 * POOL_HALO
            win4 = win2[0:n2] + win2[2:n2 + 2]
            if w == 8:
                total = win4[POOL_HALO - 4:POOL_HALO - 4 + tm] + win4[POOL_HALO:POOL_HALO + tm]
            else:
                n3 = tm + POOL_HALO
                win8 = win4[0:n3] + win4[4:n3 + 4]
                total = win8[0:tm] + win8[POOL_HALO:POOL_HALO + tm]
        lo = jnp.maximum(pos - w // 2, 0)
        hi = jnp.minimum(pos - w // 2 + w, seq)
        inv_cnt = 1.0 / (hi - lo).astype(jnp.float32)
        inv_cnt = jnp.concatenate([inv_cnt] * (POOL_GROUP // LANES), axis=-1)
        return (total * inv_cnt - center).astype(jnp.bfloat16)

    def pool_gate_task(g, w):
        def ep(z):
            pool_vals[g] = (_silu(z), pooled_group(g, w))
        return (lambda: proj(C_PG + g * POOL_GROUP, POOL_GROUP)), ep

    def pool_mix_task(g):
        cols = slice(g * POOL_GROUP, (g + 1) * POOL_GROUP)

        def mm():
            return jnp.dot(pool_vals[g][1], wpg_ref[g], preferred_element_type=jnp.float32)

        def ep(z):
            pb_ref[:, cols] = (z * ps_ref[:, cols] * pool_vals[g][0]).astype(pb_ref.dtype)
        return mm, ep

    q_scale = (HEAD_DIM ** -0.5) * LOG2_E

    def q_task(c):
        def ep(z):
            for j in range(COL_CHUNK // LANES):
                zb = rope(z[:, j * LANES:(j + 1) * LANES]) * q_scale
                q_ref[:, c + j * LANES:c + (j + 1) * LANES] = zb.astype(q_ref.dtype)
        return (lambda: proj(C_Q + c, COL_CHUNK)), ep

    def act_task(lo, out_ref, c, act):
        def ep(z):
            out_ref[:, c:c + COL_CHUNK] = act(z).astype(out_ref.dtype)
        return (lambda: proj(lo + c, COL_CHUNK)), ep

    def k_ep(zk):
        for j in range(KV_WIDTH // LANES):
            blk = rope(zk[:, j * LANES:(j + 1) * LANES])
            kt_ref[j * LANES:(j + 1) * LANES, :] = blk.T.astype(kt_ref.dtype)

    def v_ep(zv):
        v_ref[...] = zv.astype(v_ref.dtype)

    tasks = [u_task(c) for c in range(0, POOL_WIDTH, COL_CHUNK)]
    tasks += [q_task(c) for c in range(0, ATTN_WIDTH, COL_CHUNK)]
    tasks += [pool_gate_task(g, w) for g, w in enumerate(POOL_WINDOWS)]
    tasks += [(lambda: proj(C_K, KV_WIDTH), k_ep), (lambda: proj(C_V, KV_WIDTH), v_ep)]
    tasks += [pool_mix_task(g) for g in range(N_POOL_GROUPS)]
    tasks += [act_task(C_AG, ag_ref, c, _silu) for c in range(0, ATTN_WIDTH, COL_CHUNK)]
    tasks += [act_task(C_MG, gate_ref, c, jax.nn.sigmoid) for c in range(0, 2 * D_MODEL, COL_CHUNK)]
    pending = None
    for mm, ep in tasks:
        z = mm()
        if pending is not None:
            pending[0](pending[1])
        pending = (ep, z)
    pending[0](pending[1])


def _inproj_call(x, g_pre, w_in, cos, sin, wpg, ps, *, tm):
    batch, seq, _ = x.shape
    nst = seq // tm
    bf16 = jnp.bfloat16
    row_spec = lambda width: pl.BlockSpec((None, tm, width), lambda b, i: (b, i, 0))
    const = lambda shape: pl.BlockSpec(shape, lambda b, i: (0,) * len(shape))
    per = tm // POOL_HALO
    last = seq // POOL_HALO - 1
    out_shape = (
        jax.ShapeDtypeStruct((batch, seq, POOL_WIDTH), bf16),
        jax.ShapeDtypeStruct((batch, seq, ATTN_WIDTH), bf16),
        jax.ShapeDtypeStruct((batch, seq, ATTN_WIDTH), bf16),
        jax.ShapeDtypeStruct((batch, seq, 2 * D_MODEL), bf16),
        jax.ShapeDtypeStruct((batch, KV_WIDTH, seq), bf16),
        jax.ShapeDtypeStruct((batch, seq, KV_WIDTH), bf16),
    )
    out_specs = (
        row_spec(POOL_WIDTH), row_spec(ATTN_WIDTH), row_spec(ATTN_WIDTH), row_spec(2 * D_MODEL),
        pl.BlockSpec((None, KV_WIDTH, tm), lambda b, i: (b, 0, i)),
        row_spec(KV_WIDTH),
    )
    in_specs = [
        row_spec(D_MODEL),
        pl.BlockSpec((None, POOL_HALO, D_MODEL), lambda b, i: (b, jnp.maximum(i * per - 1, 0), 0)),
        pl.BlockSpec((None, POOL_HALO, D_MODEL), lambda b, i: (b, jnp.minimum((i + 1) * per, last), 0)),
        const((1, D_MODEL)),
        const((D_MODEL, IN_WIDTH)),
        pl.BlockSpec((tm, LANES), lambda b, i: (i, 0)),
        pl.BlockSpec((tm, LANES), lambda b, i: (i, 0)),
        const((N_POOL_GROUPS, POOL_GROUP, POOL_GROUP)),
        const((1, POOL_WIDTH)),
    ]
    return pl.pallas_call(
        functools.partial(_inproj_kernel, seq=seq),
        grid=(batch, nst),
        in_specs=in_specs,
        out_specs=out_specs,
        out_shape=out_shape,
        scratch_shapes=[pltpu.VMEM((tm + 2 * POOL_HALO + POOL_PAD, POOL_WIDTH), jnp.float32)],
        compiler_params=pltpu.CompilerParams(
            dimension_semantics=("arbitrary", "arbitrary"),
            vmem_limit_bytes=VMEM_LIMIT_BYTES),
        name="inproj",
    )(x, x, x, g_pre, w_in, cos, sin, wpg, ps)


def _mixer_kernel(x_ref, pb_ref, q_ref, ag_ref, gate_ref,
                  kt_ref, ktp_ref, ktn_ref, v_ref, vp_ref, vn_ref,
                  wpp_ref, sink_ref, wap_ref, wo_ref, gpost_ref,
                  y_ref,
                  kfull, vlfull, vhfull, ab_ref, pm_ref):
    tq = x_ref.shape[0]
    nqb = tq // BLOCK
    i = pl.program_id(1)
    is_first = i == 0
    is_last = i == pl.num_programs(1) - 1

    kext = jnp.concatenate([ktp_ref[...], kt_ref[...], ktn_ref[...]], axis=1)
    zero_rows = jnp.zeros((HEAD_DIM, tq + 2 * WINDOW), kfull.dtype)
    for hkv in range(N_KV_HEADS):
        kh = kext[hkv * HEAD_DIM:(hkv + 1) * HEAD_DIM, :]
        base = hkv * KV_SLOT
        kfull[base:base + HEAD_DIM, :] = kh
        kfull[base + HEAD_DIM:base + LANES, :] = zero_rows
        kfull[base + LANES:base + LANES + HEAD_DIM, :] = zero_rows
        kfull[base + LANES + HEAD_DIM:base + KV_SLOT, :] = kh
    ext_lane = lax.broadcasted_iota(jnp.int32, (tq + 2 * WINDOW, LANES), 1)
    ext_low = ext_lane < HALF_LANES
    ones_low = jnp.where(ext_low, 1.0, 0.0).astype(vlfull.dtype)
    ones_high = jnp.where(ext_low, 0.0, 1.0).astype(vhfull.dtype)
    vext = jnp.concatenate([vp_ref[...], v_ref[...], vn_ref[...]], axis=0).astype(jnp.float32)
    for j in range(KV_WIDTH // LANES):
        blk = vext[:, j * LANES:(j + 1) * LANES]
        swapped = pltpu.roll(blk, HALF_LANES, 1)
        placed = ((jnp.where(ext_low, blk, 0.0), jnp.where(ext_low, 0.0, swapped)),
                  (jnp.where(ext_low, swapped, 0.0), jnp.where(ext_low, 0.0, blk)))
        for par, (lowv, highv) in enumerate(placed):
            base = (2 * j + par) * KV_SLOT
            vlfull[:, base:base + LANES] = lowv.astype(vlfull.dtype)
            vlfull[:, base + LANES:base + KV_SLOT] = ones_low
            vhfull[:, base:base + LANES] = highv.astype(vhfull.dtype)
            vhfull[:, base + LANES:base + KV_SLOT] = ones_high

    qrow = lax.broadcasted_iota(jnp.int32, (BLOCK, LANES), 0)
    kcol = lax.broadcasted_iota(jnp.int32, (BLOCK, LANES), 1)
    low_half = kcol < HALF_LANES

    def chunk_masks(qi):
        first = kcol >= qrow
        last = kcol <= qrow
        if qi == 0:
            first = first & jnp.logical_not(is_first)
        if qi == nqb - 1:
            last = last & jnp.logical_not(is_last)
        return first, last

    masks = [chunk_masks(qi) for qi in range(nqb)]
    n_chunks = SPAN // LANES

    def scores(qi, hkv):
        rows = slice(qi * BLOCK, (qi + 1) * BLOCK)
        keys = slice(qi * BLOCK, qi * BLOCK + SPAN)
        q2 = jnp.concatenate([q_ref[rows, (2 * hkv + r) * LANES:(2 * hkv + r + 1) * LANES]
                              for r in range(GROUP // 2)], axis=0)
        kw = jnp.concatenate([kfull[hkv * KV_SLOT:hkv * KV_SLOT + LANES, keys],
                              kfull[hkv * KV_SLOT + LANES:(hkv + 1) * KV_SLOT, keys]], axis=1)
        return jnp.dot(q2, kw, preferred_element_type=jnp.float32)

    def softmax(qi, hkv, s):
        first_ok, last_ok = masks[qi]
        p_rows, sink_terms = [], []
        for r in range(GROUP // 2):
            p_cols, sink_pair = [], []
            for c in range(2):
                head = hkv * GROUP + 2 * r + c
                blk = s[r * BLOCK:(r + 1) * BLOCK, c * SPAN:(c + 1) * SPAN]
                chunks = [blk[:, k * LANES:(k + 1) * LANES] for k in range(n_chunks)]
                chunks[0] = jnp.where(first_ok, chunks[0], NEG_INF)
                chunks[-1] = jnp.where(last_ok, chunks[-1], NEG_INF)
                top = functools.reduce(jnp.maximum, chunks)
                sink_row = sink_ref[head:head + 1, :]
                m = jnp.maximum(jnp.max(top, axis=-1, keepdims=True), sink_row)
                p_cols += [jnp.exp2(ch - m).astype(jnp.bfloat16) for ch in chunks]
                sink_pair.append(jnp.exp2(sink_row - m))
            p_rows.append(jnp.concatenate(p_cols, axis=-1))
            sink_terms.append(jnp.where(low_half, sink_pair[0], sink_pair[1]))
        return jnp.concatenate(p_rows, axis=0), sink_terms

    def weighted_values(qi, hkv, p):
        keys = slice(qi * BLOCK, qi * BLOCK + SPAN)
        slot = slice(hkv * KV_SLOT, (hkv + 1) * KV_SLOT)
        vw = jnp.concatenate([vlfull[keys, slot], vhfull[keys, slot]], axis=0)
        return jnp.dot(p, vw, preferred_element_type=jnp.float32)

    def finalize(qi, hkv, o, sink_terms):
        rows = slice(qi * BLOCK, (qi + 1) * BLOCK)
        for r in range(GROUP // 2):
            lanes = slice((2 * hkv + r) * LANES, (2 * hkv + r + 1) * LANES)
            values = o[r * BLOCK:(r + 1) * BLOCK, 0:LANES]
            denom = o[r * BLOCK:(r + 1) * BLOCK, LANES:KV_SLOT] + sink_terms[r]
            gated = values / denom * ag_ref[rows, lanes].astype(jnp.float32)
            ab_ref[rows, lanes] = gated.astype(ab_ref.dtype)

    items = [(qi, hkv) for qi in range(nqb) for hkv in range(N_KV_HEADS)]

    n_fill = D_MODEL // FILL_CHUNK
    n_steps = len(items) + LAG_FINALIZE
    fill_at = {(k * len(items)) // n_fill + 1: k for k in range(n_fill)}

    def pool_fill(k):
        cols = slice(k * FILL_CHUNK, (k + 1) * FILL_CHUNK)
        pp = jnp.dot(pb_ref[...], wpp_ref[:, cols], preferred_element_type=jnp.float32)
        pm_ref[:, cols] = gate_ref[:, cols].astype(jnp.float32) * pp

    s_q, p_q, o_q = {}, {}, {}
    for t in range(n_steps):
        if t < len(items):
            s_q[t] = scores(*items[t])
        if t in fill_at:
            pool_fill(fill_at[t])
        a, b, c = t - LAG_SOFTMAX, t - LAG_VALUES, t - LAG_FINALIZE
        if 0 <= a < len(items):
            p_q[a] = softmax(*items[a], s_q.pop(a))
        if 0 <= b < len(items):
            p, sink_terms = p_q.pop(b)
            o_q[b] = (weighted_values(*items[b], p), sink_terms)
        if 0 <= c < len(items):
            finalize(*items[c], *o_q.pop(c))

    ap = jnp.dot(ab_ref[...], wap_ref[...], preferred_element_type=jnp.float32)
    merged = pm_ref[...] + gate_ref[:, D_MODEL:].astype(jnp.float32) * ap
    out = jnp.dot(merged.astype(jnp.bfloat16), wo_ref[...], preferred_element_type=jnp.float32)
    y_ref[...] = x_ref[...] + _rms_scale(out, gpost_ref[...])


def _mixer_call(x, pb, q, ag, gates, kt, v, wpp, sink, wap, wo, g_post, *, tq):
    batch, seq, _ = x.shape
    nt = seq // tq
    kvw = N_KV_HEADS * KV_SLOT
    row_spec = lambda width: pl.BlockSpec((None, tq, width), lambda b, i: (b, i, 0))
    const = lambda shape: pl.BlockSpec(shape, lambda b, i: (0,) * len(shape))
    per = tq // WINDOW
    last = seq // WINDOW - 1
    prev_blk = lambda i: jnp.maximum(i * per - 1, 0)
    next_blk = lambda i: jnp.minimum((i + 1) * per, last)
    in_specs = [
        row_spec(D_MODEL),
        row_spec(POOL_WIDTH),
        row_spec(ATTN_WIDTH),
        row_spec(ATTN_WIDTH),
        row_spec(2 * D_MODEL),
        pl.BlockSpec((None, KV_WIDTH, tq), lambda b, i: (b, 0, i)),
        pl.BlockSpec((None, KV_WIDTH, WINDOW), lambda b, i: (b, 0, prev_blk(i))),
        pl.BlockSpec((None, KV_WIDTH, WINDOW), lambda b, i: (b, 0, next_blk(i))),
        row_spec(KV_WIDTH),
        pl.BlockSpec((None, WINDOW, KV_WIDTH), lambda b, i: (b, prev_blk(i), 0)),
        pl.BlockSpec((None, WINDOW, KV_WIDTH), lambda b, i: (b, next_blk(i), 0)),
        const((POOL_WIDTH, D_MODEL)),
        const((N_HEADS, LANES)),
        const((ATTN_WIDTH, D_MODEL)),
        const((D_MODEL, D_MODEL)),
        const((1, D_MODEL)),
    ]
    scratch_shapes = [
        pltpu.VMEM((kvw, tq + 2 * WINDOW), jnp.bfloat16),
        pltpu.VMEM((tq + 2 * WINDOW, kvw), jnp.bfloat16),
        pltpu.VMEM((tq + 2 * WINDOW, kvw), jnp.bfloat16),
        pltpu.VMEM((tq, ATTN_WIDTH), jnp.bfloat16),
        pltpu.VMEM((tq, D_MODEL), jnp.float32),
    ]
    return pl.pallas_call(
        _mixer_kernel,
        grid=(batch, nt),
        in_specs=in_specs,
        out_specs=row_spec(D_MODEL),
        out_shape=jax.ShapeDtypeStruct((batch, seq, D_MODEL), jnp.float32),
        scratch_shapes=scratch_shapes,
        compiler_params=pltpu.CompilerParams(
            dimension_semantics=("arbitrary", "arbitrary"),
            vmem_limit_bytes=VMEM_LIMIT_BYTES),
        name="mixer",
    )(x, pb, q, ag, gates, kt, kt, kt, v, v, v, wpp, sink, wap, wo, g_post)


def _rope_tables(seq):
    half = HEAD_DIM // 2
    inv = ROPE_THETA ** (-jnp.arange(half, dtype=jnp.float32) / half)
    ang = jnp.arange(seq).astype(jnp.float32)[:, None] * inv[None, :]
    cos = jnp.cos(ang)
    sin = jnp.sin(ang)
    reps = LANES // HEAD_DIM
    cos_t = jnp.tile(jnp.concatenate([cos, cos], axis=-1), (1, reps))
    sin_t = jnp.tile(jnp.concatenate([-sin, sin], axis=-1), (1, reps))
    return cos_t, sin_t


def _encoder_layer(x, norm_pre, w_in, w_pool_group, pool_scale, w_pool_proj,
                   attn_sink, w_attn_proj, w_out, norm_post, *, tm=TM_INPROJ, tq=TQ_MIX):
    batch, seq, _ = x.shape
    tm = min(tm, seq)
    tq = min(tq, seq)
    assert seq % tm == 0 and seq % tq == 0 and tq % BLOCK == 0 and tm % LANES == 0
    bf16 = jnp.bfloat16
    cos_t, sin_t = _rope_tables(seq)
    pb, q, ag, gates, kt, v = _inproj_call(
        x, norm_pre.reshape(1, D_MODEL), w_in.astype(bf16), cos_t, sin_t,
        w_pool_group.astype(bf16), pool_scale.reshape(1, POOL_WIDTH).astype(jnp.float32), tm=tm)
    sink = jnp.broadcast_to((attn_sink.astype(jnp.float32) * LOG2_E)[:, None], (N_HEADS, LANES))
    return _mixer_call(
        x, pb, q, ag, gates, kt, v,
        w_pool_proj.astype(bf16), sink, w_attn_proj.astype(bf16), w_out.astype(bf16),
        norm_post.reshape(1, D_MODEL), tq=tq)


def kernel(x_prompt, x_sample, norm_pre, w_in, w_pool_group, pool_scale, w_pool_proj,
           attn_sink, w_attn_proj, w_out, norm_post):
    y_prompt, y_sample = x_prompt, x_sample
    for l in range(norm_pre.shape[0]):
        layer = functools.partial(
            _encoder_layer, norm_pre=norm_pre[l], w_in=w_in[l], w_pool_group=w_pool_group[l],
            pool_scale=pool_scale[l], w_pool_proj=w_pool_proj[l], attn_sink=attn_sink[l],
            w_attn_proj=w_attn_proj[l], w_out=w_out[l], norm_post=norm_post[l])
        y_prompt = layer(y_prompt)
        y_sample = layer(y_sample)
    return (y_prompt, y_sample)
```

```python
import functools
import math

import jax
import jax.numpy as jnp
from jax import lax
from jax.experimental import pallas as pl
from jax.experimental.pallas import tpu as pltpu

D_MODEL = 1024
N_HEADS = 16
N_KV_HEADS = 4
HEAD_DIM = 64
GROUP = N_HEADS // N_KV_HEADS
ATTN_WIDTH = N_HEADS * HEAD_DIM
KV_WIDTH = N_KV_HEADS * HEAD_DIM
WINDOW = 128
BLOCK = 128
SPAN = BLOCK + 2 * WINDOW
ROPE_THETA = 10000.0
POOL_WIDTH = D_MODEL
POOL_WINDOWS = (2, 4, 8, 16)
N_POOL_GROUPS = 4
POOL_GROUP = POOL_WIDTH // N_POOL_GROUPS
RMS_EPS = 1e-6
NEG_INF = -1e30
LOG2_E = math.log2(math.e)

C_U = 0
C_PG = C_U + POOL_WIDTH
C_Q = C_PG + POOL_WIDTH
C_K = C_Q + ATTN_WIDTH
C_V = C_K + KV_WIDTH
C_AG = C_V + KV_WIDTH
C_MG = C_AG + ATTN_WIDTH
IN_WIDTH = C_MG + 2 * D_MODEL

LANES = 128
HALF_LANES = LANES // 2
SUBLANES = 8
POOL_HALO = 8
POOL_PAD = 3 * SUBLANES
KV_SLOT = 2 * LANES
VMEM_LIMIT_BYTES = 56 * 1024 * 1024

TM_INPROJ = 512
TQ_MIX = 512
COL_CHUNK = 512
FILL_CHUNK = 256
LAG_SOFTMAX = 1
LAG_VALUES = 2
LAG_FINALIZE = 3

assert POOL_WINDOWS == (2, 4, 8, 16) and POOL_HALO == max(POOL_WINDOWS) // 2


def _silu(z):
    return z * jax.nn.sigmoid(z)


def _rms_scale(x, g):
    ms = jnp.mean(x * x, axis=-1, keepdims=True)
    return x * lax.rsqrt(ms + RMS_EPS) * g


def _inproj_kernel(x_ref, xp_ref, xn_ref, g_ref, w_ref, cos_ref, sin_ref, wpg_ref, ps_ref,
                   pb_ref, q_ref, ag_ref, gate_ref, kt_ref, vl_ref, vh_ref,
                   uext, *, seq):
    tm = x_ref.shape[0]
    i = pl.program_id(1)
    is_first = i == 0
    is_last = i == pl.num_programs(1) - 1
    g_pre = g_ref[...]
    hf = _rms_scale(x_ref[...], g_pre)
    h = hf.astype(jnp.bfloat16)

    def proj(lo, width):
        return jnp.dot(h, w_ref[:, lo:lo + width], preferred_element_type=jnp.float32)

    lane = lax.broadcasted_iota(jnp.int32, (tm, LANES), 1)
    low_half = lane < HALF_LANES
    rot_first = (lane % HEAD_DIM) < (HEAD_DIM // 2)
    cos = cos_ref[...]
    sin = sin_ref[...]

    def rope(zb):
        partner = jnp.where(rot_first, pltpu.roll(zb, LANES - HEAD_DIM // 2, 1),
                            pltpu.roll(zb, HEAD_DIM // 2, 1))
        return zb * cos + partner * sin

    h_ext = jnp.concatenate(
        [_rms_scale(xp_ref[...], g_pre), hf, _rms_scale(xn_ref[...], g_pre)], axis=0).astype(jnp.bfloat16)
    ext = tm + 2 * POOL_HALO

    def u_task(c):
        def mm():
            return jnp.dot(h_ext, w_ref[:, C_U + c:C_U + c + COL_CHUNK], preferred_element_type=jnp.float32)

        def ep(z):
            keep_top = jnp.logical_not(is_first)
            keep_bottom = jnp.logical_not(is_last)
            uext[0:POOL_HALO, c:c + COL_CHUNK] = jnp.where(keep_top, z[0:POOL_HALO], 0.0)
            uext[POOL_HALO:ext - POOL_HALO, c:c + COL_CHUNK] = z[POOL_HALO:ext - POOL_HALO]
            uext[ext - POOL_HALO:ext, c:c + COL_CHUNK] = jnp.where(keep_bottom, z[ext - POOL_HALO:ext], 0.0)
            uext[ext:ext + POOL_PAD, c:c + COL_CHUNK] = jnp.zeros((POOL_PAD, COL_CHUNK), jnp.float32)
        return mm, ep

    pos = i * tm + lax.broadcasted_iota(jnp.int32, (tm, LANES), 0)
    pool_vals = {}

    def pooled_group(g, w):
        cols = slice(g * POOL_GROUP, (g + 1) * POOL_GROUP)
        u_all = uext[:, cols]
        center = u_all[POOL_HALO:POOL_HALO + tm]
        n1 = tm + 2 * POOL_HALO + SUBLANES
        win2 = u_all[0:n1] + u_all[1:n1 + 1]
        if w == 2:
            total = win2[POOL_HALO - 1:POOL_HALO - 1 + tm]
        elif w == 4:
            total = win2[POOL_HALO - 2:POOL_HALO - 2 + tm] + win2[POOL_HALO:POOL_HALO + tm]
        else:
            n2 = tm + 2 * POOL_HALO
            win4 = win2[0:n2] + win2[2:n2 + 2]
            if w == 8:
                total = win4[POOL_HALO - 4:POOL_HALO - 4 + tm] + win4[POOL_HALO:POOL_HALO + tm]
            else:
                n3 = tm + POOL_HALO
                win8 = win4[0:n3] + win4[4:n3 + 4]
                total = win8[0:tm] + win8[POOL_HALO:POOL_HALO + tm]
        lo = jnp.maximum(pos - w // 2, 0)
        hi = jnp.minimum(pos - w // 2 + w, seq)
        inv_cnt = 1.0 / (hi - lo).astype(jnp.float32)
        inv_cnt = jnp.concatenate([inv_cnt] * (POOL_GROUP // LANES), axis=-1)
        return (total * inv_cnt - center).astype(jnp.bfloat16)

    def pool_gate_task(g, w):
        def ep(z):
            pool_vals[g] = (_silu(z), pooled_group(g, w))
        return (lambda: proj(C_PG + g * POOL_GROUP, POOL_GROUP)), ep

    def pool_mix_task(g):
        cols = slice(g * POOL_GROUP, (g + 1) * POOL_GROUP)

        def mm():
            return jnp.dot(pool_vals[g][1], wpg_ref[g], preferred_element_type=jnp.float32)

        def ep(z):
            pb_ref[:, cols] = (z * ps_ref[:, cols] * pool_vals[g][0]).astype(pb_ref.dtype)
        return mm, ep

    q_scale = (HEAD_DIM ** -0.5) * LOG2_E

    def q_task(c):
        def ep(z):
            for j in range(COL_CHUNK // LANES):
                zb = rope(z[:, j * LANES:(j + 1) * LANES]) * q_scale
                q_ref[:, c + j * LANES:c + (j + 1) * LANES] = zb.astype(q_ref.dtype)
        return (lambda: proj(C_Q + c, COL_CHUNK)), ep

    def act_task(lo, out_ref, c, act):
        def ep(z):
            out_ref[:, c:c + COL_CHUNK] = act(z).astype(out_ref.dtype)
        return (lambda: proj(lo + c, COL_CHUNK)), ep

    def k_ep(zk):
        for j in range(KV_WIDTH // LANES):
            blk = rope(zk[:, j * LANES:(j + 1) * LANES])
            kt_ref[j * LANES:(j + 1) * LANES, :] = blk.T.astype(kt_ref.dtype)

    def v_ep(zv):
        for j in range(KV_WIDTH // LANES):
            blk = zv[:, j * LANES:(j + 1) * LANES]
            swapped = pltpu.roll(blk, HALF_LANES, 1)
            placed = ((jnp.where(low_half, blk, 0.0), jnp.where(low_half, 0.0, swapped)),
                      (jnp.where(low_half, swapped, 0.0), jnp.where(low_half, 0.0, blk)))
            for par, (lowv, highv) in enumerate(placed):
                lanes = slice((2 * j + par) * LANES, (2 * j + par + 1) * LANES)
                vl_ref[:, lanes] = lowv.astype(vl_ref.dtype)
                vh_ref[:, lanes] = highv.astype(vh_ref.dtype)

    tasks = [u_task(c) for c in range(0, POOL_WIDTH, COL_CHUNK)]
    tasks += [q_task(c) for c in range(0, ATTN_WIDTH, COL_CHUNK)]
    tasks += [pool_gate_task(g, w) for g, w in enumerate(POOL_WINDOWS)]
    tasks += [(lambda: proj(C_K, KV_WIDTH), k_ep), (lambda: proj(C_V, KV_WIDTH), v_ep)]
    tasks += [pool_mix_task(g) for g in range(N_POOL_GROUPS)]
    tasks += [act_task(C_AG, ag_ref, c, _silu) for c in range(0, ATTN_WIDTH, COL_CHUNK)]
    tasks += [act_task(C_MG, gate_ref, c, jax.nn.sigmoid) for c in range(0, 2 * D_MODEL, COL_CHUNK)]
    pending = None
    for mm, ep in tasks:
        z = mm()
        if pending is not None:
            pending[0](pending[1])
        pending = (ep, z)
    pending[0](pending[1])


def _inproj_call(x, g_pre, w_in, cos, sin, wpg, ps, *, tm):
    batch, seq, _ = x.shape
    nst = seq // tm
    bf16 = jnp.bfloat16
    kvw = N_KV_HEADS * LANES
    row_spec = lambda width: pl.BlockSpec((None, tm, width), lambda b, i: (b, i, 0))
    const = lambda shape: pl.BlockSpec(shape, lambda b, i: (0,) * len(shape))
    per = tm // POOL_HALO
    last = seq // POOL_HALO - 1
    out_shape = (
        jax.ShapeDtypeStruct((batch, seq, POOL_WIDTH), bf16),
        jax.ShapeDtypeStruct((batch, seq, ATTN_WIDTH), bf16),
        jax.ShapeDtypeStruct((batch, seq, ATTN_WIDTH), bf16),
        jax.ShapeDtypeStruct((batch, seq, 2 * D_MODEL), bf16),
        jax.ShapeDtypeStruct((batch, KV_WIDTH, seq), bf16),
        jax.ShapeDtypeStruct((batch, seq, kvw), bf16),
        jax.ShapeDtypeStruct((batch, seq, kvw), bf16),
    )
    out_specs = (
        row_spec(POOL_WIDTH), row_spec(ATTN_WIDTH), row_spec(ATTN_WIDTH), row_spec(2 * D_MODEL),
        pl.BlockSpec((None, KV_WIDTH, tm), lambda b, i: (b, 0, i)),
        row_spec(kvw), row_spec(kvw),
    )
    in_specs = [
        row_spec(D_MODEL),
        pl.BlockSpec((None, POOL_HALO, D_MODEL), lambda b, i: (b, jnp.maximum(i * per - 1, 0), 0)),
        pl.BlockSpec((None, POOL_HALO, D_MODEL), lambda b, i: (b, jnp.minimum((i + 1) * per, last), 0)),
        const((1, D_MODEL)),
        const((D_MODEL, IN_WIDTH)),
        pl.BlockSpec((tm, LANES), lambda b, i: (i, 0)),
        pl.BlockSpec((tm, LANES), lambda b, i: (i, 0)),
        const((N_POOL_GROUPS, POOL_GROUP, POOL_GROUP)),
        const((1, POOL_WIDTH)),
    ]
    return pl.pallas_call(
        functools.partial(_inproj_kernel, seq=seq),
        grid=(batch, nst),
        in_specs=in_specs,
        out_specs=out_specs,
        out_shape=out_shape,
        scratch_shapes=[pltpu.VMEM((tm + 2 * POOL_HALO + POOL_PAD, POOL_WIDTH), jnp.float32)],
        compiler_params=pltpu.CompilerParams(
            dimension_semantics=("arbitrary", "arbitrary"),
            vmem_limit_bytes=VMEM_LIMIT_BYTES),
        name="inproj",
    )(x, x, x, g_pre, w_in, cos, sin, wpg, ps)


def _mixer_kernel(x_ref, pb_ref, q_ref, ag_ref, gate_ref,
                  kt_ref, ktp_ref, ktn_ref, vl_ref, vlp_ref, vln_ref, vh_ref, vhp_ref, vhn_ref,
                  wpp_ref, sink_ref, wap_ref, wo_ref, gpost_ref,
                  y_ref,
                  kfull, vlfull, vhfull, ab_ref, pm_ref):
    tq = x_ref.shape[0]
    nqb = tq // BLOCK
    i = pl.program_id(1)
    is_first = i == 0
    is_last = i == pl.num_programs(1) - 1

    kext = jnp.concatenate([ktp_ref[...], kt_ref[...], ktn_ref[...]], axis=1)
    zero_rows = jnp.zeros((HEAD_DIM, tq + 2 * WINDOW), kfull.dtype)
    for hkv in range(N_KV_HEADS):
        kh = kext[hkv * HEAD_DIM:(hkv + 1) * HEAD_DIM, :]
        base = hkv * KV_SLOT
        kfull[base:base + HEAD_DIM, :] = kh
        kfull[base + HEAD_DIM:base + LANES, :] = zero_rows
        kfull[base + LANES:base + LANES + HEAD_DIM, :] = zero_rows
        kfull[base + LANES + HEAD_DIM:base + KV_SLOT, :] = kh
    ext_low = lax.broadcasted_iota(jnp.int32, (tq + 2 * WINDOW, LANES), 1) < HALF_LANES
    ones_low = jnp.where(ext_low, 1.0, 0.0).astype(vlfull.dtype)
    ones_high = jnp.where(ext_low, 0.0, 1.0).astype(vhfull.dtype)
    for full, prev, cur, nxt, ones in ((vlfull, vlp_ref, vl_ref, vln_ref, ones_low),
                                       (vhfull, vhp_ref, vh_ref, vhn_ref, ones_high)):
        for hkv in range(N_KV_HEADS):
            src = slice(hkv * LANES, (hkv + 1) * LANES)
            base = hkv * KV_SLOT
            full[0:WINDOW, base:base + LANES] = prev[:, src]
            full[WINDOW:WINDOW + tq, base:base + LANES] = cur[:, src]
            full[WINDOW + tq:, base:base + LANES] = nxt[:, src]
            full[:, base + LANES:base + KV_SLOT] = ones

    qrow = lax.broadcasted_iota(jnp.int32, (BLOCK, LANES), 0)
    kcol = lax.broadcasted_iota(jnp.int32, (BLOCK, LANES), 1)
    low_half = kcol < HALF_LANES

    def chunk_masks(qi):
        first = kcol >= qrow
        last = kcol <= qrow
        if qi == 0:
            first = first & jnp.logical_not(is_first)
        if qi == nqb - 1:
            last = last & jnp.logical_not(is_last)
        return first, last

    masks = [chunk_masks(qi) for qi in range(nqb)]
    n_chunks = SPAN // LANES

    def scores(qi, hkv):
        rows = slice(qi * BLOCK, (qi + 1) * BLOCK)
        keys = slice(qi * BLOCK, qi * BLOCK + SPAN)
        q2 = jnp.concatenate([q_ref[rows, (2 * hkv + r) * LANES:(2 * hkv + r + 1) * LANES]
                              for r in range(GROUP // 2)], axis=0)
        kw = jnp.concatenate([kfull[hkv * KV_SLOT:hkv * KV_SLOT + LANES, keys],
                              kfull[hkv * KV_SLOT + LANES:(hkv + 1) * KV_SLOT, keys]], axis=1)
        return jnp.dot(q2, kw, preferred_element_type=jnp.float32)

    def softmax(qi, hkv, s):
        first_ok, last_ok = masks[qi]
        p_rows, sink_terms = [], []
        for r in range(GROUP // 2):
            p_cols, sink_pair = [], []
            for c in range(2):
                head = hkv * GROUP + 2 * r + c
                blk = s[r * BLOCK:(r + 1) * BLOCK, c * SPAN:(c + 1) * SPAN]
                chunks = [blk[:, k * LANES:(k + 1) * LANES] for k in range(n_chunks)]
                chunks[0] = jnp.where(first_ok, chunks[0], NEG_INF)
                chunks[-1] = jnp.where(last_ok, chunks[-1], NEG_INF)
                top = functools.reduce(jnp.maximum, chunks)
                sink_row = sink_ref[head:head + 1, :]
                m = jnp.maximum(jnp.max(top, axis=-1, keepdims=True), sink_row)
                p_cols += [jnp.exp2(ch - m).astype(jnp.bfloat16) for ch in chunks]
                sink_pair.append(jnp.exp2(sink_row - m))
            p_rows.append(jnp.concatenate(p_cols, axis=-1))
            sink_terms.append(jnp.where(low_half, sink_pair[0], sink_pair[1]))
        return jnp.concatenate(p_rows, axis=0), sink_terms

    def weighted_values(qi, hkv, p):
        keys = slice(qi * BLOCK, qi * BLOCK + SPAN)
        slot = slice(hkv * KV_SLOT, (hkv + 1) * KV_SLOT)
        vw = jnp.concatenate([vlfull[keys, slot], vhfull[keys, slot]], axis=0)
        return jnp.dot(p, vw, preferred_element_type=jnp.float32)

    def finalize(qi, hkv, o, sink_terms):
        rows = slice(qi * BLOCK, (qi + 1) * BLOCK)
        for r in range(GROUP // 2):
            lanes = slice((2 * hkv + r) * LANES, (2 * hkv + r + 1) * LANES)
            values = o[r * BLOCK:(r + 1) * BLOCK, 0:LANES]
            denom = o[r * BLOCK:(r + 1) * BLOCK, LANES:KV_SLOT] + sink_terms[r]
            gated = values / denom * ag_ref[rows, lanes].astype(jnp.float32)
            ab_ref[rows, lanes] = gated.astype(ab_ref.dtype)

    items = [(qi, hkv) for qi in range(nqb) for hkv in range(N_KV_HEADS)]

    n_fill = D_MODEL // FILL_CHUNK
    n_steps = len(items) + LAG_FINALIZE
    fill_at = {(k * len(items)) // n_fill + 1: k for k in range(n_fill)}

    def pool_fill(k):
        cols = slice(k * FILL_CHUNK, (k + 1) * FILL_CHUNK)
        pp = jnp.dot(pb_ref[...], wpp_ref[:, cols], preferred_element_type=jnp.float32)
        pm_ref[:, cols] = gate_ref[:, cols].astype(jnp.float32) * pp

    s_q, p_q, o_q = {}, {}, {}
    for t in range(n_steps):
        if t < len(items):
            s_q[t] = scores(*items[t])
        if t in fill_at:
            pool_fill(fill_at[t])
        a, b, c = t - LAG_SOFTMAX, t - LAG_VALUES, t - LAG_FINALIZE
        if 0 <= a < len(items):
            p_q[a] = softmax(*items[a], s_q.pop(a))
        if 0 <= b < len(items):
            p, sink_terms = p_q.pop(b)
            o_q[b] = (weighted_values(*items[b], p), sink_terms)
        if 0 <= c < len(items):
            finalize(*items[c], *o_q.pop(c))

    ap = jnp.dot(ab_ref[...], wap_ref[...], preferred_element_type=jnp.float32)
    merged = pm_ref[...] + gate_ref[:, D_MODEL:].astype(jnp.float32) * ap
    out = jnp.dot(merged.astype(jnp.bfloat16), wo_ref[...], preferred_element_type=jnp.float32)
    y_ref[...] = x_ref[...] + _rms_scale(out, gpost_ref[...])


def _mixer_call(x, pb, q, ag, gates, kt, vl, vh, wpp, sink, wap, wo, g_post, *, tq):
    batch, seq, _ = x.shape
    nt = seq // tq
    kvw = N_KV_HEADS * KV_SLOT
    row_spec = lambda width: pl.BlockSpec((None, tq, width), lambda b, i: (b, i, 0))
    const = lambda shape: pl.BlockSpec(shape, lambda b, i: (0,) * len(shape))
    per = tq // WINDOW
    last = seq // WINDOW - 1
    prev_blk = lambda i: jnp.maximum(i * per - 1, 0)
    next_blk = lambda i: jnp.minimum((i + 1) * per, last)
    vw = N_KV_HEADS * LANES
    v_specs = [row_spec(vw),
               pl.BlockSpec((None, WINDOW, vw), lambda b, i: (b, prev_blk(i), 0)),
               pl.BlockSpec((None, WINDOW, vw), lambda b, i: (b, next_blk(i), 0))]
    in_specs = [
        row_spec(D_MODEL),
        row_spec(POOL_WIDTH),
        row_spec(ATTN_WIDTH),
        row_spec(ATTN_WIDTH),
        row_spec(2 * D_MODEL),
        pl.BlockSpec((None, KV_WIDTH, tq), lambda b, i: (b, 0, i)),
        pl.BlockSpec((None, KV_WIDTH, WINDOW), lambda b, i: (b, 0, prev_blk(i))),
        pl.BlockSpec((None, KV_WIDTH, WINDOW), lambda b, i: (b, 0, next_blk(i))),
        *v_specs, *v_specs,
        const((POOL_WIDTH, D_MODEL)),
        const((N_HEADS, LANES)),
        const((ATTN_WIDTH, D_MODEL)),
        const((D_MODEL, D_MODEL)),
        const((1, D_MODEL)),
    ]
    scratch_shapes = [
        pltpu.VMEM((kvw, tq + 2 * WINDOW), jnp.bfloat16),
        pltpu.VMEM((tq + 2 * WINDOW, kvw), jnp.bfloat16),
        pltpu.VMEM((tq + 2 * WINDOW, kvw), jnp.bfloat16),
        pltpu.VMEM((tq, ATTN_WIDTH), jnp.bfloat16),
        pltpu.VMEM((tq, D_MODEL), jnp.float32),
    ]
    return pl.pallas_call(
        _mixer_kernel,
        grid=(batch, nt),
        in_specs=in_specs,
        out_specs=row_spec(D_MODEL),
        out_shape=jax.ShapeDtypeStruct((batch, seq, D_MODEL), jnp.float32),
        scratch_shapes=scratch_shapes,
        compiler_params=pltpu.CompilerParams(
            dimension_semantics=("arbitrary", "arbitrary"),
            vmem_limit_bytes=VMEM_LIMIT_BYTES),
        name="mixer",
    )(x, pb, q, ag, gates, kt, kt, kt, vl, vl, vl, vh, vh, vh, wpp, sink, wap, wo, g_post)


def _rope_tables(seq):
    half = HEAD_DIM // 2
    inv = ROPE_THETA ** (-jnp.arange(half, dtype=jnp.float32) / half)
    ang = jnp.arange(seq).astype(jnp.float32)[:, None] * inv[None, :]
    cos = jnp.cos(ang)
    sin = jnp.sin(ang)
    reps = LANES // HEAD_DIM
    cos_t = jnp.tile(jnp.concatenate([cos, cos], axis=-1), (1, reps))
    sin_t = jnp.tile(jnp.concatenate([-sin, sin], axis=-1), (1, reps))
    return cos_t, sin_t


def _encoder_layer(x, norm_pre, w_in, w_pool_group, pool_scale, w_pool_proj,
                   attn_sink, w_attn_proj, w_out, norm_post, *, tm=TM_INPROJ, tq=TQ_MIX):
    batch, seq, _ = x.shape
    tm = min(tm, seq)
    tq = min(tq, seq)
    assert seq % tm == 0 and seq % tq == 0 and tq % BLOCK == 0 and tm % LANES == 0
    bf16 = jnp.bfloat16
    cos_t, sin_t = _rope_tables(seq)
    pb, q, ag, gates, kt, vl, vh = _inproj_call(
        x, norm_pre.reshape(1, D_MODEL), w_in.astype(bf16), cos_t, sin_t,
        w_pool_group.astype(bf16), pool_scale.reshape(1, POOL_WIDTH).astype(jnp.float32), tm=tm)
    sink = jnp.broadcast_to((attn_sink.astype(jnp.float32) * LOG2_E)[:, None], (N_HEADS, LANES))
    return _mixer_call(
        x, pb, q, ag, gates, kt, vl, vh,
        w_pool_proj.astype(bf16), sink, w_attn_proj.astype(bf16), w_out.astype(bf16),
        norm_post.reshape(1, D_MODEL), tq=tq)


def kernel(x_prompt, x_sample, norm_pre, w_in, w_pool_group, pool_scale, w_pool_proj,
           attn_sink, w_attn_proj, w_out, norm_post):
    y_prompt, y_sample = x_prompt, x_sample
    for l in range(norm_pre.shape[0]):
        layer = functools.partial(
            _encoder_layer, norm_pre=norm_pre[l], w_in=w_in[l], w_pool_group=w_pool_group[l],
            pool_scale=pool_scale[l], w_pool_proj=w_pool_proj[l], attn_sink=attn_sink[l],
            w_attn_proj=w_attn_proj[l], w_out=w_out[l], norm_post=norm_post[l])
        y_prompt = layer(y_prompt)
        y_sample = layer(y_sample)
    return (y_prompt, y_sample)
```

```python
import functools
import math

import jax
import jax.numpy as jnp
from jax import lax
from jax.experimental import pallas as pl
from jax.experimental.pallas import tpu as pltpu

D_MODEL = 1024
N_HEADS = 16
N_KV_HEADS = 4
HEAD_DIM = 64
GROUP = N_HEADS // N_KV_HEADS
ATTN_WIDTH = N_HEADS * HEAD_DIM
KV_WIDTH = N_KV_HEADS * HEAD_DIM
WINDOW = 128
BLOCK = 128
SPAN = BLOCK + 2 * WINDOW
ROPE_THETA = 10000.0
POOL_WIDTH = D_MODEL
POOL_WINDOWS = (2, 4, 8, 16)
N_POOL_GROUPS = 4
POOL_GROUP = POOL_WIDTH // N_POOL_GROUPS
RMS_EPS = 1e-6
NEG_INF = -1e30
LOG2_E = math.log2(math.e)

C_U = 0
C_PG = C_U + POOL_WIDTH
C_Q = C_PG + POOL_WIDTH
C_K = C_Q + ATTN_WIDTH
C_V = C_K + KV_WIDTH
C_AG = C_V + KV_WIDTH
C_MG = C_AG + ATTN_WIDTH
IN_WIDTH = C_MG + 2 * D_MODEL

LANES = 128
HALF_LANES = LANES // 2
SUBLANES = 8
POOL_HALO = 8
POOL_PAD = 3 * SUBLANES
KV_SLOT = 2 * LANES
VMEM_LIMIT_BYTES = 56 * 1024 * 1024

TM_INPROJ = 512
TQ_MIX = 512
COL_CHUNK = 512
FILL_CHUNK = 256
LAG_SOFTMAX = 1
LAG_VALUES = 2
LAG_FINALIZE = 3

assert POOL_WINDOWS == (2, 4, 8, 16) and POOL_HALO == max(POOL_WINDOWS) // 2


def _silu(z):
    return z * jax.nn.sigmoid(z)


def _rms_scale(x, g):
    ms = jnp.mean(x * x, axis=-1, keepdims=True)
    return x * lax.rsqrt(ms + RMS_EPS) * g


def _inproj_kernel(x_ref, xp_ref, xn_ref, g_ref, w_ref, cos_ref, sin_ref, wpg_ref, ps_ref,
                   pb_ref, q_ref, ag_ref, gate_ref, kt_ref, vl_ref, vh_ref,
                   uext, *, seq):
    tm = x_ref.shape[0]
    i = pl.program_id(1)
    is_first = i == 0
    is_last = i == pl.num_programs(1) - 1
    g_pre = g_ref[...]
    hf = _rms_scale(x_ref[...], g_pre)
    h = hf.astype(jnp.bfloat16)

    def proj(lo, width):
        return jnp.dot(h, w_ref[:, lo:lo + width], preferred_element_type=jnp.float32)

    lane = lax.broadcasted_iota(jnp.int32, (tm, LANES), 1)
    low_half = lane < HALF_LANES
    rot_first = (lane % HEAD_DIM) < (HEAD_DIM // 2)
    cos = cos_ref[...]
    sin = sin_ref[...]

    def rope(zb):
        partner = jnp.where(rot_first, pltpu.roll(zb, LANES - HEAD_DIM // 2, 1),
                            pltpu.roll(zb, HEAD_DIM // 2, 1))
        return zb * cos + partner * sin

    def split_heads(blk):
        swapped = pltpu.roll(blk, HALF_LANES, 1)
        return ((jnp.where(low_half, blk, 0.0), jnp.where(low_half, 0.0, swapped)),
                (jnp.where(low_half, swapped, 0.0), jnp.where(low_half, 0.0, blk)))

    h_ext = jnp.concatenate(
        [_rms_scale(xp_ref[...], g_pre), hf, _rms_scale(xn_ref[...], g_pre)], axis=0).astype(jnp.bfloat16)
    ext = tm + 2 * POOL_HALO

    def u_task(c):
        def mm():
            return jnp.dot(h_ext, w_ref[:, C_U + c:C_U + c + COL_CHUNK], preferred_element_type=jnp.float32)

        def ep(z):
            keep_top = jnp.logical_not(is_first)
            keep_bottom = jnp.logical_not(is_last)
            uext[0:POOL_HALO, c:c + COL_CHUNK] = jnp.where(keep_top, z[0:POOL_HALO], 0.0)
            uext[POOL_HALO:ext - POOL_HALO, c:c + COL_CHUNK] = z[POOL_HALO:ext - POOL_HALO]
            uext[ext - POOL_HALO:ext, c:c + COL_CHUNK] = jnp.where(keep_bottom, z[ext - POOL_HALO:ext], 0.0)
            uext[ext:ext + POOL_PAD, c:c + COL_CHUNK] = jnp.zeros((POOL_PAD, COL_CHUNK), jnp.float32)
        return mm, ep

    pos = i * tm + lax.broadcasted_iota(jnp.int32, (tm, LANES), 0)
    pool_vals = {}

    def pooled_group(g, w):
        cols = slice(g * POOL_GROUP, (g + 1) * POOL_GROUP)
        u_all = uext[:, cols]
        center = u_all[POOL_HALO:POOL_HALO + tm]
        n1 = tm + 2 * POOL_HALO + SUBLANES
        win2 = u_all[0:n1] + u_all[1:n1 + 1]
        if w == 2:
            total = win2[POOL_HALO - 1:POOL_HALO - 1 + tm]
        elif w == 4:
            total = win2[POOL_HALO - 2:POOL_HALO - 2 + tm] + win2[POOL_HALO:POOL_HALO + tm]
        else:
            n2 = tm + 2 * POOL_HALO
            win4 = win2[0:n2] + win2[2:n2 + 2]
            if w == 8:
                total = win4[POOL_HALO - 4:POOL_HALO - 4 + tm] + win4[POOL_HALO:POOL_HALO + tm]
            else:
                n3 = tm + POOL_HALO
                win8 = win4[0:n3] + win4[4:n3 + 4]
                total = win8[0:tm] + win8[POOL_HALO:POOL_HALO + tm]
        lo = jnp.maximum(pos - w // 2, 0)
        hi = jnp.minimum(pos - w // 2 + w, seq)
        inv_cnt = 1.0 / (hi - lo).astype(jnp.float32)
        inv_cnt = jnp.concatenate([inv_cnt] * (POOL_GROUP // LANES), axis=-1)
        return (total * inv_cnt - center).astype(jnp.bfloat16)

    def pool_gate_task(g, w):
        def ep(z):
            pool_vals[g] = (_silu(z), pooled_group(g, w))
        return (lambda: proj(C_PG + g * POOL_GROUP, POOL_GROUP)), ep

    def pool_mix_task(g):
        cols = slice(g * POOL_GROUP, (g + 1) * POOL_GROUP)

        def mm():
            return jnp.dot(pool_vals[g][1], wpg_ref[g], preferred_element_type=jnp.float32)

        def ep(z):
            pb_ref[:, cols] = (z * ps_ref[:, cols] * pool_vals[g][0]).astype(pb_ref.dtype)
        return mm, ep

    q_scale = (HEAD_DIM ** -0.5) * LOG2_E

    def q_task(c):
        def ep(z):
            for j in range(COL_CHUNK // LANES):
                zb = rope(z[:, j * LANES:(j + 1) * LANES]) * q_scale
                q_ref[:, c + j * LANES:c + (j + 1) * LANES] = zb.astype(q_ref.dtype)
        return (lambda: proj(C_Q + c, COL_CHUNK)), ep

    def act_task(lo, out_ref, c, act):
        def ep(z):
            out_ref[:, c:c + COL_CHUNK] = act(z).astype(out_ref.dtype)
        return (lambda: proj(lo + c, COL_CHUNK)), ep

    def k_ep(zk):
        for j in range(KV_WIDTH // LANES):
            for par, (lowv, highv) in enumerate(split_heads(rope(zk[:, j * LANES:(j + 1) * LANES]))):
                base = (2 * j + par) * KV_SLOT
                kt_ref[base:base + LANES, :] = lowv.T.astype(kt_ref.dtype)
                kt_ref[base + LANES:base + KV_SLOT, :] = highv.T.astype(kt_ref.dtype)

    def v_ep(zv):
        for j in range(KV_WIDTH // LANES):
            for par, (lowv, highv) in enumerate(split_heads(zv[:, j * LANES:(j + 1) * LANES])):
                lanes = slice((2 * j + par) * LANES, (2 * j + par + 1) * LANES)
                vl_ref[:, lanes] = lowv.astype(vl_ref.dtype)
                vh_ref[:, lanes] = highv.astype(vh_ref.dtype)

    tasks = [u_task(c) for c in range(0, POOL_WIDTH, COL_CHUNK)]
    tasks += [q_task(c) for c in range(0, ATTN_WIDTH, COL_CHUNK)]
    tasks += [pool_gate_task(g, w) for g, w in enumerate(POOL_WINDOWS)]
    tasks += [(lambda: proj(C_K, KV_WIDTH), k_ep), (lambda: proj(C_V, KV_WIDTH), v_ep)]
    tasks += [pool_mix_task(g) for g in range(N_POOL_GROUPS)]
    tasks += [act_task(C_AG, ag_ref, c, _silu) for c in range(0, ATTN_WIDTH, COL_CHUNK)]
    tasks += [act_task(C_MG, gate_ref, c, jax.nn.sigmoid) for c in range(0, 2 * D_MODEL, COL_CHUNK)]
    pending = None
    for mm, ep in tasks:
        z = mm()
        if pending is not None:
            pending[0](pending[1])
        pending = (ep, z)
    pending[0](pending[1])


def _inproj_call(x, g_pre, w_in, cos, sin, wpg, ps, *, tm):
    batch, seq, _ = x.shape
    nst = seq // tm
    bf16 = jnp.bfloat16
    kvw = N_KV_HEADS * KV_SLOT
    row_spec = lambda width: pl.BlockSpec((None, tm, width), lambda b, i: (b, i, 0))
    const = lambda shape: pl.BlockSpec(shape, lambda b, i: (0,) * len(shape))
    per = tm // POOL_HALO
    last = seq // POOL_HALO - 1
    out_shape = (
        jax.ShapeDtypeStruct((batch, seq, POOL_WIDTH), bf16),
        jax.ShapeDtypeStruct((batch, seq, ATTN_WIDTH), bf16),
        jax.ShapeDtypeStruct((batch, seq, ATTN_WIDTH), bf16),
        jax.ShapeDtypeStruct((batch, seq, 2 * D_MODEL), bf16),
        jax.ShapeDtypeStruct((batch, kvw, seq), bf16),
        jax.ShapeDtypeStruct((batch, seq, N_KV_HEADS * LANES), bf16),
        jax.ShapeDtypeStruct((batch, seq, N_KV_HEADS * LANES), bf16),
    )
    out_specs = (
        row_spec(POOL_WIDTH), row_spec(ATTN_WIDTH), row_spec(ATTN_WIDTH), row_spec(2 * D_MODEL),
        pl.BlockSpec((None, kvw, tm), lambda b, i: (b, 0, i)),
        row_spec(N_KV_HEADS * LANES), row_spec(N_KV_HEADS * LANES),
    )
    in_specs = [
        row_spec(D_MODEL),
        pl.BlockSpec((None, POOL_HALO, D_MODEL), lambda b, i: (b, jnp.maximum(i * per - 1, 0), 0)),
        pl.BlockSpec((None, POOL_HALO, D_MODEL), lambda b, i: (b, jnp.minimum((i + 1) * per, last), 0)),
        const((1, D_MODEL)),
        const((D_MODEL, IN_WIDTH)),
        pl.BlockSpec((tm, LANES), lambda b, i: (i, 0)),
        pl.BlockSpec((tm, LANES), lambda b, i: (i, 0)),
        const((N_POOL_GROUPS, POOL_GROUP, POOL_GROUP)),
        const((1, POOL_WIDTH)),
    ]
    return pl.pallas_call(
        functools.partial(_inproj_kernel, seq=seq),
        grid=(batch, nst),
        in_specs=in_specs,
        out_specs=out_specs,
        out_shape=out_shape,
        scratch_shapes=[pltpu.VMEM((tm + 2 * POOL_HALO + POOL_PAD, POOL_WIDTH), jnp.float32)],
        compiler_params=pltpu.CompilerParams(
            dimension_semantics=("arbitrary", "arbitrary"),
            vmem_limit_bytes=VMEM_LIMIT_BYTES),
        name="inproj",
    )(x, x, x, g_pre, w_in, cos, sin, wpg, ps)


def _mixer_kernel(x_ref, pb_ref, q_ref, ag_ref, gate_ref,
                  kt_ref, ktp_ref, ktn_ref, vl_ref, vlp_ref, vln_ref, vh_ref, vhp_ref, vhn_ref,
                  wpp_ref, sink_ref, wap_ref, wo_ref, gpost_ref,
                  y_ref,
                  kfull, vlfull, vhfull, ab_ref, pm_ref):
    tq = x_ref.shape[0]
    nqb = tq // BLOCK
    i = pl.program_id(1)
    is_first = i == 0
    is_last = i == pl.num_programs(1) - 1

    kfull[:, 0:WINDOW] = ktp_ref[...]
    kfull[:, WINDOW:WINDOW + tq] = kt_ref[...]
    kfull[:, WINDOW + tq:] = ktn_ref[...]
    ext_low = lax.broadcasted_iota(jnp.int32, (tq + 2 * WINDOW, LANES), 1) < HALF_LANES
    ones_low = jnp.where(ext_low, 1.0, 0.0).astype(vlfull.dtype)
    ones_high = jnp.where(ext_low, 0.0, 1.0).astype(vhfull.dtype)
    for full, prev, cur, nxt, ones in ((vlfull, vlp_ref, vl_ref, vln_ref, ones_low),
                                       (vhfull, vhp_ref, vh_ref, vhn_ref, ones_high)):
        for hkv in range(N_KV_HEADS):
            src = slice(hkv * LANES, (hkv + 1) * LANES)
            base = hkv * KV_SLOT
            full[0:WINDOW, base:base + LANES] = prev[:, src]
            full[WINDOW:WINDOW + tq, base:base + LANES] = cur[:, src]
            full[WINDOW + tq:, base:base + LANES] = nxt[:, src]
            full[:, base + LANES:base + KV_SLOT] = ones

    qrow = lax.broadcasted_iota(jnp.int32, (BLOCK, LANES), 0)
    kcol = lax.broadcasted_iota(jnp.int32, (BLOCK, LANES), 1)
    low_half = kcol < HALF_LANES

    def chunk_masks(qi):
        first = kcol >= qrow
        last = kcol <= qrow
        if qi == 0:
            first = first & jnp.logical_not(is_first)
        if qi == nqb - 1:
            last = last & jnp.logical_not(is_last)
        return first, last

    masks = [chunk_masks(qi) for qi in range(nqb)]
    n_chunks = SPAN // LANES

    def scores(qi, hkv):
        rows = slice(qi * BLOCK, (qi + 1) * BLOCK)
        keys = slice(qi * BLOCK, qi * BLOCK + SPAN)
        q2 = jnp.concatenate([q_ref[rows, (2 * hkv + r) * LANES:(2 * hkv + r + 1) * LANES]
                              for r in range(GROUP // 2)], axis=0)
        kw = jnp.concatenate([kfull[hkv * KV_SLOT:hkv * KV_SLOT + LANES, keys],
                              kfull[hkv * KV_SLOT + LANES:(hkv + 1) * KV_SLOT, keys]], axis=1)
        return jnp.dot(q2, kw, preferred_element_type=jnp.float32)

    def softmax(qi, hkv, s):
        first_ok, last_ok = masks[qi]
        p_rows, sink_terms = [], []
        for r in range(GROUP // 2):
            p_cols, sink_pair = [], []
            for c in range(2):
                head = hkv * GROUP + 2 * r + c
                blk = s[r * BLOCK:(r + 1) * BLOCK, c * SPAN:(c + 1) * SPAN]
                chunks = [blk[:, k * LANES:(k + 1) * LANES] for k in range(n_chunks)]
                chunks[0] = jnp.where(first_ok, chunks[0], NEG_INF)
                chunks[-1] = jnp.where(last_ok, chunks[-1], NEG_INF)
                top = functools.reduce(jnp.maximum, chunks)
                sink_row = sink_ref[head:head + 1, :]
                m = jnp.maximum(jnp.max(top, axis=-1, keepdims=True), sink_row)
                p_cols += [jnp.exp2(ch - m).astype(jnp.bfloat16) for ch in chunks]
                sink_pair.append(jnp.exp2(sink_row - m))
            p_rows.append(jnp.concatenate(p_cols, axis=-1))
            sink_terms.append(jnp.where(low_half, sink_pair[0], sink_pair[1]))
        return jnp.concatenate(p_rows, axis=0), sink_terms

    def weighted_values(qi, hkv, p):
        keys = slice(qi * BLOCK, qi * BLOCK + SPAN)
        slot = slice(hkv * KV_SLOT, (hkv + 1) * KV_SLOT)
        vw = jnp.concatenate([vlfull[keys, slot], vhfull[keys, slot]], axis=0)
        return jnp.dot(p, vw, preferred_element_type=jnp.float32)

    def finalize(qi, hkv, o, sink_terms):
        rows = slice(qi * BLOCK, (qi + 1) * BLOCK)
        for r in range(GROUP // 2):
            lanes = slice((2 * hkv + r) * LANES, (2 * hkv + r + 1) * LANES)
            values = o[r * BLOCK:(r + 1) * BLOCK, 0:LANES]
            denom = o[r * BLOCK:(r + 1) * BLOCK, LANES:KV_SLOT] + sink_terms[r]
            gated = values / denom * ag_ref[rows, lanes].astype(jnp.float32)
            ab_ref[rows, lanes] = gated.astype(ab_ref.dtype)

    items = [(qi, hkv) for qi in range(nqb) for hkv in range(N_KV_HEADS)]

    n_fill = D_MODEL // FILL_CHUNK
    n_steps = len(items) + LAG_FINALIZE
    fill_at = {(k * len(items)) // n_fill + 1: k for k in range(n_fill)}

    def pool_fill(k):
        cols = slice(k * FILL_CHUNK, (k + 1) * FILL_CHUNK)
        pp = jnp.dot(pb_ref[...], wpp_ref[:, cols], preferred_element_type=jnp.float32)
        pm_ref[:, cols] = gate_ref[:, cols].astype(jnp.float32) * pp

    s_q, p_q, o_q = {}, {}, {}
    for t in range(n_steps):
        if t < len(items):
            s_q[t] = scores(*items[t])
        if t in fill_at:
            pool_fill(fill_at[t])
        a, b, c = t - LAG_SOFTMAX, t - LAG_VALUES, t - LAG_FINALIZE
        if 0 <= a < len(items):
            p_q[a] = softmax(*items[a], s_q.pop(a))
        if 0 <= b < len(items):
            p, sink_terms = p_q.pop(b)
            o_q[b] = (weighted_values(*items[b], p), sink_terms)
        if 0 <= c < len(items):
            finalize(*items[c], *o_q.pop(c))

    ap = jnp.dot(ab_ref[...], wap_ref[...], preferred_element_type=jnp.float32)
    merged = pm_ref[...] + gate_ref[:, D_MODEL:].astype(jnp.float32) * ap
    out = jnp.dot(merged.astype(jnp.bfloat16), wo_ref[...], preferred_element_type=jnp.float32)
    y_ref[...] = x_ref[...] + _rms_scale(out, gpost_ref[...])


def _mixer_call(x, pb, q, ag, gates, kt, vl, vh, wpp, sink, wap, wo, g_post, *, tq):
    batch, seq, _ = x.shape
    nt = seq // tq
    kvw = N_KV_HEADS * KV_SLOT
    row_spec = lambda width: pl.BlockSpec((None, tq, width), lambda b, i: (b, i, 0))
    const = lambda shape: pl.BlockSpec(shape, lambda b, i: (0,) * len(shape))
    per = tq // WINDOW
    last = seq // WINDOW - 1
    prev_blk = lambda i: jnp.maximum(i * per - 1, 0)
    next_blk = lambda i: jnp.minimum((i + 1) * per, last)
    vw = N_KV_HEADS * LANES
    v_specs = [row_spec(vw),
               pl.BlockSpec((None, WINDOW, vw), lambda b, i: (b, prev_blk(i), 0)),
               pl.BlockSpec((None, WINDOW, vw), lambda b, i: (b, next_blk(i), 0))]
    in_specs = [
        row_spec(D_MODEL),
        row_spec(POOL_WIDTH),
        row_spec(ATTN_WIDTH),
        row_spec(ATTN_WIDTH),
        row_spec(2 * D_MODEL),
        pl.BlockSpec((None, kvw, tq), lambda b, i: (b, 0, i)),
        pl.BlockSpec((None, kvw, WINDOW), lambda b, i: (b, 0, prev_blk(i))),
        pl.BlockSpec((None, kvw, WINDOW), lambda b, i: (b, 0, next_blk(i))),
        *v_specs, *v_specs,
        const((POOL_WIDTH, D_MODEL)),
        const((N_HEADS, LANES)),
        const((ATTN_WIDTH, D_MODEL)),
        const((D_MODEL, D_MODEL)),
        const((1, D_MODEL)),
    ]
    scratch_shapes = [
        pltpu.VMEM((kvw, tq + 2 * WINDOW), jnp.bfloat16),
        pltpu.VMEM((tq + 2 * WINDOW, kvw), jnp.bfloat16),
        pltpu.VMEM((tq + 2 * WINDOW, kvw), jnp.bfloat16),
        pltpu.VMEM((tq, ATTN_WIDTH), jnp.bfloat16),
        pltpu.VMEM((tq, D_MODEL), jnp.float32),
    ]
    return pl.pallas_call(
        _mixer_kernel,
        grid=(batch, nt),
        in_specs=in_specs,
        out_specs=row_spec(D_MODEL),
        out_shape=jax.ShapeDtypeStruct((batch, seq, D_MODEL), jnp.float32),
        scratch_shapes=scratch_shapes,
        compiler_params=pltpu.CompilerParams(
            dimension_semantics=("arbitrary", "arbitrary"),
            vmem_limit_bytes=VMEM_LIMIT_BYTES),
        name="mixer",
    )(x, pb, q, ag, gates, kt, kt, kt, vl, vl, vl, vh, vh, vh, wpp, sink, wap, wo, g_post)


def _rope_tables(seq):
    half = HEAD_DIM // 2
    inv = ROPE_THETA ** (-jnp.arange(half, dtype=jnp.float32) / half)
    ang = jnp.arange(seq).astype(jnp.float32)[:, None] * inv[None, :]
    cos = jnp.cos(ang)
    sin = jnp.sin(ang)
    reps = LANES // HEAD_DIM
    cos_t = jnp.tile(jnp.concatenate([cos, cos], axis=-1), (1, reps))
    sin_t = jnp.tile(jnp.concatenate([-sin, sin], axis=-1), (1, reps))
    return cos_t, sin_t


def _encoder_layer(x, norm_pre, w_in, w_pool_group, pool_scale, w_pool_proj,
                   attn_sink, w_attn_proj, w_out, norm_post, *, tm=TM_INPROJ, tq=TQ_MIX):
    batch, seq, _ = x.shape
    tm = min(tm, seq)
    tq = min(tq, seq)
    assert seq % tm == 0 and seq % tq == 0 and tq % BLOCK == 0 and tm % LANES == 0
    bf16 = jnp.bfloat16
    cos_t, sin_t = _rope_tables(seq)
    pb, q, ag, gates, kt, vl, vh = _inproj_call(
        x, norm_pre.reshape(1, D_MODEL), w_in.astype(bf16), cos_t, sin_t,
        w_pool_group.astype(bf16), pool_scale.reshape(1, POOL_WIDTH).astype(jnp.float32), tm=tm)
    sink = jnp.broadcast_to((attn_sink.astype(jnp.float32) * LOG2_E)[:, None], (N_HEADS, LANES))
    return _mixer_call(
        x, pb, q, ag, gates, kt, vl, vh,
        w_pool_proj.astype(bf16), sink, w_attn_proj.astype(bf16), w_out.astype(bf16),
        norm_post.reshape(1, D_MODEL), tq=tq)


def kernel(x_prompt, x_sample, norm_pre, w_in, w_pool_group, pool_scale, w_pool_proj,
           attn_sink, w_attn_proj, w_out, norm_post):
    y_prompt, y_sample = x_prompt, x_sample
    for l in range(norm_pre.shape[0]):
        layer = functools.partial(
            _encoder_layer, norm_pre=norm_pre[l], w_in=w_in[l], w_pool_group=w_pool_group[l],
            pool_scale=pool_scale[l], w_pool_proj=w_pool_proj[l], attn_sink=attn_sink[l],
            w_attn_proj=w_attn_proj[l], w_out=w_out[l], norm_post=norm_post[l])
        y_prompt = layer(y_prompt)
        y_sample = layer(y_sample)
    return (y_prompt, y_sample)
```

```python
import functools
import math

import jax
import jax.numpy as jnp
from jax import lax
from jax.experimental import pallas as pl
from jax.experimental.pallas import tpu as pltpu

D_MODEL = 1024
N_HEADS = 16
N_KV_HEADS = 4
HEAD_DIM = 64
GROUP = N_HEADS // N_KV_HEADS
ATTN_WIDTH = N_HEADS * HEAD_DIM
KV_WIDTH = N_KV_HEADS * HEAD_DIM
WINDOW = 128
BLOCK = 128
SPAN = BLOCK + 2 * WINDOW
ROPE_THETA = 10000.0
POOL_WIDTH = D_MODEL
POOL_WINDOWS = (2, 4, 8, 16)
N_POOL_GROUPS = 4
POOL_GROUP = POOL_WIDTH // N_POOL_GROUPS
RMS_EPS = 1e-6
NEG_INF = -1e30
LOG2_E = math.log2(math.e)

C_U = 0
C_PG = C_U + POOL_WIDTH
C_Q = C_PG + POOL_WIDTH
C_K = C_Q + ATTN_WIDTH
C_V = C_K + KV_WIDTH
C_AG = C_V + KV_WIDTH
C_MG = C_AG + ATTN_WIDTH
IN_WIDTH = C_MG + 2 * D_MODEL

LANES = 128
HALF_LANES = LANES // 2
SUBLANES = 8
POOL_HALO = 8
POOL_PAD = 3 * SUBLANES
KV_SLOT = 2 * LANES
VMEM_LIMIT_BYTES = 56 * 1024 * 1024

TM_INPROJ = 512
TQ_MIX = 512
COL_CHUNK = 512
FILL_CHUNK = 256
LAG_SOFTMAX = 1
LAG_VALUES = 2
LAG_FINALIZE = 3

assert POOL_WINDOWS == (2, 4, 8, 16) and POOL_HALO == max(POOL_WINDOWS) // 2


def _silu(z):
    return z * jax.nn.sigmoid(z)


def _rms_scale(x, g):
    ms = jnp.mean(x * x, axis=-1, keepdims=True)
    return x * lax.rsqrt(ms + RMS_EPS) * g


def _inproj_kernel(x_ref, xp_ref, xn_ref, g_ref, w_ref, cos_ref, sin_ref, wpg_ref, ps_ref,
                   pb_ref, q_ref, ag_ref, gate_ref, kt_ref, vl_ref, vh_ref,
                   uext, *, seq):
    tm = x_ref.shape[0]
    i = pl.program_id(1)
    is_first = i == 0
    is_last = i == pl.num_programs(1) - 1
    g_pre = g_ref[...]
    hf = _rms_scale(x_ref[...], g_pre)
    h = hf.astype(jnp.bfloat16)

    def proj(lo, width):
        return jnp.dot(h, w_ref[:, lo:lo + width], preferred_element_type=jnp.float32)

    lane = lax.broadcasted_iota(jnp.int32, (tm, LANES), 1)
    low_half = lane < HALF_LANES
    rot_first = (lane % HEAD_DIM) < (HEAD_DIM // 2)
    cos = cos_ref[...]
    sin = sin_ref[...]

    def rope(zb):
        partner = jnp.where(rot_first, pltpu.roll(zb, LANES - HEAD_DIM // 2, 1),
                            pltpu.roll(zb, HEAD_DIM // 2, 1))
        return zb * cos + partner * sin

    def split_heads(blk):
        swapped = pltpu.roll(blk, HALF_LANES, 1)
        return ((jnp.where(low_half, blk, 0.0), jnp.where(low_half, 0.0, swapped)),
                (jnp.where(low_half, swapped, 0.0), jnp.where(low_half, 0.0, blk)))

    h_ext = jnp.concatenate(
        [_rms_scale(xp_ref[...], g_pre), hf, _rms_scale(xn_ref[...], g_pre)], axis=0).astype(jnp.bfloat16)
    ext = tm + 2 * POOL_HALO

    def u_task(c):
        def mm():
            return jnp.dot(h_ext, w_ref[:, C_U + c:C_U + c + COL_CHUNK], preferred_element_type=jnp.float32)

        def ep(z):
            keep_top = jnp.logical_not(is_first)
            keep_bottom = jnp.logical_not(is_last)
            uext[0:POOL_HALO, c:c + COL_CHUNK] = jnp.where(keep_top, z[0:POOL_HALO], 0.0)
            uext[POOL_HALO:ext - POOL_HALO, c:c + COL_CHUNK] = z[POOL_HALO:ext - POOL_HALO]
            uext[ext - POOL_HALO:ext, c:c + COL_CHUNK] = jnp.where(keep_bottom, z[ext - POOL_HALO:ext], 0.0)
            uext[ext:ext + POOL_PAD, c:c + COL_CHUNK] = jnp.zeros((POOL_PAD, COL_CHUNK), jnp.float32)
        return mm, ep

    pos = i * tm + lax.broadcasted_iota(jnp.int32, (tm, LANES), 0)
    pool_vals = {}

    def pooled_group(g, w):
        cols = slice(g * POOL_GROUP, (g + 1) * POOL_GROUP)
        u_all = uext[:, cols]
        center = u_all[POOL_HALO:POOL_HALO + tm]
        n1 = tm + 2 * POOL_HALO + SUBLANES
        win2 = u_all[0:n1] + u_all[1:n1 + 1]
        if w == 2:
            total = win2[POOL_HALO - 1:POOL_HALO - 1 + tm]
        elif w == 4:
            total = win2[POOL_HALO - 2:POOL_HALO - 2 + tm] + win2[POOL_HALO:POOL_HALO + tm]
        else:
            n2 = tm + 2 * POOL_HALO
            win4 = win2[0:n2] + win2[2:n2 + 2]
            if w == 8:
                total = win4[POOL_HALO - 4:POOL_HALO - 4 + tm] + win4[POOL_HALO:POOL_HALO + tm]
            else:
                n3 = tm + POOL_HALO
                win8 = win4[0:n3] + win4[4:n3 + 4]
                total = win8[0:tm] + win8[POOL_HALO:POOL_HALO + tm]
        lo = jnp.maximum(pos - w // 2, 0)
        hi = jnp.minimum(pos - w // 2 + w, seq)
        inv_cnt = 1.0 / (hi - lo).astype(jnp.float32)
        inv_cnt = jnp.concatenate([inv_cnt] * (POOL_GROUP // LANES), axis=-1)
        return (total * inv_cnt - center).astype(jnp.bfloat16)

    def pool_gate_task(g, w):
        def ep(z):
            pool_vals[g] = (_silu(z), pooled_group(g, w))
        return (lambda: proj(C_PG + g * POOL_GROUP, POOL_GROUP)), ep

    def pool_mix_task(g):
        cols = slice(g * POOL_GROUP, (g + 1) * POOL_GROUP)

        def mm():
            return jnp.dot(pool_vals[g][1], wpg_ref[g], preferred_element_type=jnp.float32)

        def ep(z):
            pb_ref[:, cols] = (z * ps_ref[:, cols] * pool_vals[g][0]).astype(pb_ref.dtype)
        return mm, ep

    q_scale = (HEAD_DIM ** -0.5) * LOG2_E

    def q_task(c):
        def ep(z):
            for j in range(COL_CHUNK // LANES):
                zb = rope(z[:, j * LANES:(j + 1) * LANES]) * q_scale
                q_ref[:, c + j * LANES:c + (j + 1) * LANES] = zb.astype(q_ref.dtype)
        return (lambda: proj(C_Q + c, COL_CHUNK)), ep

    def act_task(lo, out_ref, c, act):
        def ep(z):
            out_ref[:, c:c + COL_CHUNK] = act(z).astype(out_ref.dtype)
        return (lambda: proj(lo + c, COL_CHUNK)), ep

    def k_ep(zk):
        for j in range(KV_WIDTH // LANES):
            for par, (lowv, highv) in enumerate(split_heads(rope(zk[:, j * LANES:(j + 1) * LANES]))):
                base = (2 * j + par) * KV_SLOT
                kt_ref[base:base + LANES, :] = lowv.T.astype(kt_ref.dtype)
                kt_ref[base + LANES:base + KV_SLOT, :] = highv.T.astype(kt_ref.dtype)

    def v_ep(zv):
        ones_low = jnp.where(low_half, 1.0, 0.0).astype(vl_ref.dtype)
        ones_high = jnp.where(low_half, 0.0, 1.0).astype(vh_ref.dtype)
        for j in range(KV_WIDTH // LANES):
            for par, (lowv, highv) in enumerate(split_heads(zv[:, j * LANES:(j + 1) * LANES])):
                base = (2 * j + par) * KV_SLOT
                vl_ref[:, base:base + LANES] = lowv.astype(vl_ref.dtype)
                vl_ref[:, base + LANES:base + KV_SLOT] = ones_low
                vh_ref[:, base:base + LANES] = highv.astype(vh_ref.dtype)
                vh_ref[:, base + LANES:base + KV_SLOT] = ones_high

    tasks = [u_task(c) for c in range(0, POOL_WIDTH, COL_CHUNK)]
    tasks += [q_task(c) for c in range(0, ATTN_WIDTH, COL_CHUNK)]
    tasks += [pool_gate_task(g, w) for g, w in enumerate(POOL_WINDOWS)]
    tasks += [(lambda: proj(C_K, KV_WIDTH), k_ep), (lambda: proj(C_V, KV_WIDTH), v_ep)]
    tasks += [pool_mix_task(g) for g in range(N_POOL_GROUPS)]
    tasks += [act_task(C_AG, ag_ref, c, _silu) for c in range(0, ATTN_WIDTH, COL_CHUNK)]
    tasks += [act_task(C_MG, gate_ref, c, jax.nn.sigmoid) for c in range(0, 2 * D_MODEL, COL_CHUNK)]
    pending = None
    for mm, ep in tasks:
        z = mm()
        if pending is not None:
            pending[0](pending[1])
        pending = (ep, z)
    pending[0](pending[1])


def _inproj_call(x, g_pre, w_in, cos, sin, wpg, ps, *, tm):
    batch, seq, _ = x.shape
    nst = seq // tm
    bf16 = jnp.bfloat16
    kvw = N_KV_HEADS * KV_SLOT
    row_spec = lambda width: pl.BlockSpec((None, tm, width), lambda b, i: (b, i, 0))
    const = lambda shape: pl.BlockSpec(shape, lambda b, i: (0,) * len(shape))
    per = tm // POOL_HALO
    last = seq // POOL_HALO - 1
    out_shape = (
        jax.ShapeDtypeStruct((batch, seq, POOL_WIDTH), bf16),
        jax.ShapeDtypeStruct((batch, seq, ATTN_WIDTH), bf16),
        jax.ShapeDtypeStruct((batch, seq, ATTN_WIDTH), bf16),
        jax.ShapeDtypeStruct((batch, seq, 2 * D_MODEL), bf16),
        jax.ShapeDtypeStruct((batch, kvw, seq), bf16),
        jax.ShapeDtypeStruct((batch, seq, kvw), bf16),
        jax.ShapeDtypeStruct((batch, seq, kvw), bf16),
    )
    out_specs = (
        row_spec(POOL_WIDTH), row_spec(ATTN_WIDTH), row_spec(ATTN_WIDTH), row_spec(2 * D_MODEL),
        pl.BlockSpec((None, kvw, tm), lambda b, i: (b, 0, i)),
        row_spec(kvw), row_spec(kvw),
    )
    in_specs = [
        row_spec(D_MODEL),
        pl.BlockSpec((None, POOL_HALO, D_MODEL), lambda b, i: (b, jnp.maximum(i * per - 1, 0), 0)),
        pl.BlockSpec((None, POOL_HALO, D_MODEL), lambda b, i: (b, jnp.minimum((i + 1) * per, last), 0)),
        const((1, D_MODEL)),
        const((D_MODEL, IN_WIDTH)),
        pl.BlockSpec((tm, LANES), lambda b, i: (i, 0)),
        pl.BlockSpec((tm, LANES), lambda b, i: (i, 0)),
        const((N_POOL_GROUPS, POOL_GROUP, POOL_GROUP)),
        const((1, POOL_WIDTH)),
    ]
    return pl.pallas_call(
        functools.partial(_inproj_kernel, seq=seq),
        grid=(batch, nst),
        in_specs=in_specs,
        out_specs=out_specs,
        out_shape=out_shape,
        scratch_shapes=[pltpu.VMEM((tm + 2 * POOL_HALO + POOL_PAD, POOL_WIDTH), jnp.float32)],
        compiler_params=pltpu.CompilerParams(
            dimension_semantics=("arbitrary", "arbitrary"),
            vmem_limit_bytes=VMEM_LIMIT_BYTES),
        name="inproj",
    )(x, x, x, g_pre, w_in, cos, sin, wpg, ps)


def _mixer_kernel(x_ref, pb_ref, q_ref, ag_ref, gate_ref,
                  kt_ref, ktp_ref, ktn_ref, vl_ref, vlp_ref, vln_ref, vh_ref, vhp_ref, vhn_ref,
                  wpp_ref, sink_ref, wap_ref, wo_ref, gpost_ref,
                  y_ref,
                  kfull, vlfull, vhfull, ab_ref, pm_ref):
    tq = x_ref.shape[0]
    nqb = tq // BLOCK
    i = pl.program_id(1)
    is_first = i == 0
    is_last = i == pl.num_programs(1) - 1

    kfull[:, 0:WINDOW] = ktp_ref[...]
    kfull[:, WINDOW:WINDOW + tq] = kt_ref[...]
    kfull[:, WINDOW + tq:] = ktn_ref[...]
    for full, prev, cur, nxt in ((vlfull, vlp_ref, vl_ref, vln_ref), (vhfull, vhp_ref, vh_ref, vhn_ref)):
        width = prev.shape[1]
        full[0:WINDOW, 0:width] = prev[...]
        full[WINDOW:WINDOW + tq, 0:width] = cur[...]
        full[WINDOW + tq:, 0:width] = nxt[...]

    qrow = lax.broadcasted_iota(jnp.int32, (BLOCK, LANES), 0)
    kcol = lax.broadcasted_iota(jnp.int32, (BLOCK, LANES), 1)
    low_half = kcol < HALF_LANES

    def chunk_masks(qi):
        first = kcol >= qrow
        last = kcol <= qrow
        if qi == 0:
            first = first & jnp.logical_not(is_first)
        if qi == nqb - 1:
            last = last & jnp.logical_not(is_last)
        return first, last

    masks = [chunk_masks(qi) for qi in range(nqb)]
    n_chunks = SPAN // LANES

    def scores(qi, hkv):
        rows = slice(qi * BLOCK, (qi + 1) * BLOCK)
        keys = slice(qi * BLOCK, qi * BLOCK + SPAN)
        q2 = jnp.concatenate([q_ref[rows, (2 * hkv + r) * LANES:(2 * hkv + r + 1) * LANES]
                              for r in range(GROUP // 2)], axis=0)
        kw = jnp.concatenate([kfull[hkv * KV_SLOT:hkv * KV_SLOT + LANES, keys],
                              kfull[hkv * KV_SLOT + LANES:(hkv + 1) * KV_SLOT, keys]], axis=1)
        return jnp.dot(q2, kw, preferred_element_type=jnp.float32)

    def softmax(qi, hkv, s):
        first_ok, last_ok = masks[qi]
        p_rows, sink_terms = [], []
        for r in range(GROUP // 2):
            p_cols, sink_pair = [], []
            for c in range(2):
                head = hkv * GROUP + 2 * r + c
                blk = s[r * BLOCK:(r + 1) * BLOCK, c * SPAN:(c + 1) * SPAN]
                chunks = [blk[:, k * LANES:(k + 1) * LANES] for k in range(n_chunks)]
                chunks[0] = jnp.where(first_ok, chunks[0], NEG_INF)
                chunks[-1] = jnp.where(last_ok, chunks[-1], NEG_INF)
                top = functools.reduce(jnp.maximum, chunks)
                sink_row = sink_ref[head:head + 1, :]
                m = jnp.maximum(jnp.max(top, axis=-1, keepdims=True), sink_row)
                p_cols += [jnp.exp2(ch - m).astype(jnp.bfloat16) for ch in chunks]
                sink_pair.append(jnp.exp2(sink_row - m))
            p_rows.append(jnp.concatenate(p_cols, axis=-1))
            sink_terms.append(jnp.where(low_half, sink_pair[0], sink_pair[1]))
        return jnp.concatenate(p_rows, axis=0), sink_terms

    def weighted_values(qi, hkv, p):
        keys = slice(qi * BLOCK, qi * BLOCK + SPAN)
        slot = slice(hkv * KV_SLOT, (hkv + 1) * KV_SLOT)
        vw = jnp.concatenate([vlfull[keys, slot], vhfull[keys, slot]], axis=0)
        return jnp.dot(p, vw, preferred_element_type=jnp.float32)

    def finalize(qi, hkv, o, sink_terms):
        rows = slice(qi * BLOCK, (qi + 1) * BLOCK)
        for r in range(GROUP // 2):
            lanes = slice((2 * hkv + r) * LANES, (2 * hkv + r + 1) * LANES)
            values = o[r * BLOCK:(r + 1) * BLOCK, 0:LANES]
            denom = o[r * BLOCK:(r + 1) * BLOCK, LANES:KV_SLOT] + sink_terms[r]
            gated = values / denom * ag_ref[rows, lanes].astype(jnp.float32)
            ab_ref[rows, lanes] = gated.astype(ab_ref.dtype)

    items = [(qi, hkv) for qi in range(nqb) for hkv in range(N_KV_HEADS)]

    n_fill = D_MODEL // FILL_CHUNK
    n_steps = len(items) + LAG_FINALIZE
    fill_at = {(k * len(items)) // n_fill + 1: k for k in range(n_fill)}

    def pool_fill(k):
        cols = slice(k * FILL_CHUNK, (k + 1) * FILL_CHUNK)
        pp = jnp.dot(pb_ref[...], wpp_ref[:, cols], preferred_element_type=jnp.float32)
        pm_ref[:, cols] = gate_ref[:, cols].astype(jnp.float32) * pp

    s_q, p_q, o_q = {}, {}, {}
    for t in range(n_steps):
        if t < len(items):
            s_q[t] = scores(*items[t])
        if t in fill_at:
            pool_fill(fill_at[t])
        a, b, c = t - LAG_SOFTMAX, t - LAG_VALUES, t - LAG_FINALIZE
        if 0 <= a < len(items):
            p_q[a] = softmax(*items[a], s_q.pop(a))
        if 0 <= b < len(items):
            p, sink_terms = p_q.pop(b)
            o_q[b] = (weighted_values(*items[b], p), sink_terms)
        if 0 <= c < len(items):
            finalize(*items[c], *o_q.pop(c))

    ap = jnp.dot(ab_ref[:, 0:ATTN_WIDTH], wap_ref[...], preferred_element_type=jnp.float32)
    merged = pm_ref[:, 0:D_MODEL] + gate_ref[:, D_MODEL:].astype(jnp.float32) * ap
    out = jnp.dot(merged.astype(jnp.bfloat16), wo_ref[...], preferred_element_type=jnp.float32)
    y_ref[...] = x_ref[...] + _rms_scale(out, gpost_ref[...])


def _mixer_call(x, pb, q, ag, gates, kt, vl, vh, wpp, sink, wap, wo, g_post, *, tq):
    batch, seq, _ = x.shape
    nt = seq // tq
    kvw = N_KV_HEADS * KV_SLOT
    row_spec = lambda width: pl.BlockSpec((None, tq, width), lambda b, i: (b, i, 0))
    const = lambda shape: pl.BlockSpec(shape, lambda b, i: (0,) * len(shape))
    per = tq // WINDOW
    last = seq // WINDOW - 1
    prev_blk = lambda i: jnp.maximum(i * per - 1, 0)
    next_blk = lambda i: jnp.minimum((i + 1) * per, last)
    v_specs = [row_spec(kvw),
               pl.BlockSpec((None, WINDOW, kvw), lambda b, i: (b, prev_blk(i), 0)),
               pl.BlockSpec((None, WINDOW, kvw), lambda b, i: (b, next_blk(i), 0))]
    in_specs = [
        row_spec(D_MODEL),
        row_spec(POOL_WIDTH),
        row_spec(ATTN_WIDTH),
        row_spec(ATTN_WIDTH),
        row_spec(2 * D_MODEL),
        pl.BlockSpec((None, kvw, tq), lambda b, i: (b, 0, i)),
        pl.BlockSpec((None, kvw, WINDOW), lambda b, i: (b, 0, prev_blk(i))),
        pl.BlockSpec((None, kvw, WINDOW), lambda b, i: (b, 0, next_blk(i))),
        *v_specs, *v_specs,
        const((POOL_WIDTH, D_MODEL)),
        const((N_HEADS, LANES)),
        const((ATTN_WIDTH, D_MODEL)),
        const((D_MODEL, D_MODEL)),
        const((1, D_MODEL)),
    ]
    scratch_shapes = [
        pltpu.VMEM((kvw, tq + 2 * WINDOW), jnp.bfloat16),
        pltpu.VMEM((tq + 2 * WINDOW, kvw + LANES), jnp.bfloat16),
        pltpu.VMEM((tq + 2 * WINDOW, kvw + LANES), jnp.bfloat16),
        pltpu.VMEM((tq, ATTN_WIDTH + LANES), jnp.bfloat16),
        pltpu.VMEM((tq, D_MODEL + LANES), jnp.float32),
    ]
    return pl.pallas_call(
        _mixer_kernel,
        grid=(batch, nt),
        in_specs=in_specs,
        out_specs=row_spec(D_MODEL),
        out_shape=jax.ShapeDtypeStruct((batch, seq, D_MODEL), jnp.float32),
        scratch_shapes=scratch_shapes,
        compiler_params=pltpu.CompilerParams(
            dimension_semantics=("arbitrary", "arbitrary"),
            vmem_limit_bytes=VMEM_LIMIT_BYTES),
        name="mixer",
    )(x, pb, q, ag, gates, kt, kt, kt, vl, vl, vl, vh, vh, vh, wpp, sink, wap, wo, g_post)


def _rope_tables(seq):
    half = HEAD_DIM // 2
    inv = ROPE_THETA ** (-jnp.arange(half, dtype=jnp.float32) / half)
    ang = jnp.arange(seq).astype(jnp.float32)[:, None] * inv[None, :]
    cos = jnp.cos(ang)
    sin = jnp.sin(ang)
    reps = LANES // HEAD_DIM
    cos_t = jnp.tile(jnp.concatenate([cos, cos], axis=-1), (1, reps))
    sin_t = jnp.tile(jnp.concatenate([-sin, sin], axis=-1), (1, reps))
    return cos_t, sin_t


def _encoder_layer(x, norm_pre, w_in, w_pool_group, pool_scale, w_pool_proj,
                   attn_sink, w_attn_proj, w_out, norm_post, *, tm=TM_INPROJ, tq=TQ_MIX):
    batch, seq, _ = x.shape
    tm = min(tm, seq)
    tq = min(tq, seq)
    assert seq % tm == 0 and seq % tq == 0 and tq % BLOCK == 0 and tm % LANES == 0
    bf16 = jnp.bfloat16
    cos_t, sin_t = _rope_tables(seq)
    pb, q, ag, gates, kt, vl, vh = _inproj_call(
        x, norm_pre.reshape(1, D_MODEL), w_in.astype(bf16), cos_t, sin_t,
        w_pool_group.astype(bf16), pool_scale.reshape(1, POOL_WIDTH).astype(jnp.float32), tm=tm)
    sink = jnp.broadcast_to((attn_sink.astype(jnp.float32) * LOG2_E)[:, None], (N_HEADS, LANES))
    return _mixer_call(
        x, pb, q, ag, gates, kt, vl, vh,
        w_pool_proj.astype(bf16), sink, w_attn_proj.astype(bf16), w_out.astype(bf16),
        norm_post.reshape(1, D_MODEL), tq=tq)


def kernel(x_prompt, x_sample, norm_pre, w_in, w_pool_group, pool_scale, w_pool_proj,
           attn_sink, w_attn_proj, w_out, norm_post):
    y_prompt, y_sample = x_prompt, x_sample
    for l in range(norm_pre.shape[0]):
        layer = functools.partial(
            _encoder_layer, norm_pre=norm_pre[l], w_in=w_in[l], w_pool_group=w_pool_group[l],
            pool_scale=pool_scale[l], w_pool_proj=w_pool_proj[l], attn_sink=attn_sink[l],
            w_attn_proj=w_attn_proj[l], w_out=w_out[l], norm_post=norm_post[l])
        y_prompt = layer(y_prompt)
        y_sample = layer(y_sample)
    return (y_prompt, y_sample)
```

```python
import functools
import math

import jax
import jax.numpy as jnp
from jax import lax
from jax.experimental import pallas as pl
from jax.experimental.pallas import tpu as pltpu

D_MODEL = 1024
N_HEADS = 16
N_KV_HEADS = 4
HEAD_DIM = 64
GROUP = N_HEADS // N_KV_HEADS
ATTN_WIDTH = N_HEADS * HEAD_DIM
KV_WIDTH = N_KV_HEADS * HEAD_DIM
WINDOW = 128
BLOCK = 128
SPAN = BLOCK + 2 * WINDOW
ROPE_THETA = 10000.0
POOL_WIDTH = D_MODEL
POOL_WINDOWS = (2, 4, 8, 16)
N_POOL_GROUPS = 4
POOL_GROUP = POOL_WIDTH // N_POOL_GROUPS
RMS_EPS = 1e-6
NEG_INF = -1e30
LOG2_E = math.log2(math.e)

C_U = 0
C_PG = C_U + POOL_WIDTH
C_Q = C_PG + POOL_WIDTH
C_K = C_Q + ATTN_WIDTH
C_V = C_K + KV_WIDTH
C_AG = C_V + KV_WIDTH
C_MG = C_AG + ATTN_WIDTH
IN_WIDTH = C_MG + 2 * D_MODEL

LANES = 128
HALF_LANES = LANES // 2
SUBLANES = 8
POOL_HALO = 8
POOL_PAD = 3 * SUBLANES
KV_SLOT = 2 * LANES
VMEM_LIMIT_BYTES = 56 * 1024 * 1024

TM_INPROJ = 512
TQ_MIX = 512
COL_CHUNK = 512
FILL_CHUNK = 256
LAG_SOFTMAX = 1
LAG_VALUES = 2
LAG_FINALIZE = 3

assert POOL_WINDOWS == (2, 4, 8, 16) and POOL_HALO == max(POOL_WINDOWS) // 2


def _silu(z):
    return z * jax.nn.sigmoid(z)


def _rms_scale(x, g):
    ms = jnp.mean(x * x, axis=-1, keepdims=True)
    return x * lax.rsqrt(ms + RMS_EPS) * g


def _inproj_kernel(x_ref, xp_ref, xn_ref, g_ref, w_ref, cos_ref, sin_ref, wpg_ref, ps_ref,
                   pb_ref, q_ref, ag_ref, gate_ref, kt_ref, vl_ref, vh_ref,
                   uext, *, seq):
    tm = x_ref.shape[0]
    i = pl.program_id(1)
    is_first = i == 0
    is_last = i == pl.num_programs(1) - 1
    g_pre = g_ref[...]
    hf = _rms_scale(x_ref[...], g_pre)
    h = hf.astype(jnp.bfloat16)

    def proj(lo, width):
        return jnp.dot(h, w_ref[:, lo:lo + width], preferred_element_type=jnp.float32)

    lane = lax.broadcasted_iota(jnp.int32, (tm, LANES), 1)
    low_half = lane < HALF_LANES
    rot_first = (lane % HEAD_DIM) < (HEAD_DIM // 2)
    cos = cos_ref[...]
    sin = sin_ref[...]

    def rope(zb):
        partner = jnp.where(rot_first, pltpu.roll(zb, LANES - HEAD_DIM // 2, 1),
                            pltpu.roll(zb, HEAD_DIM // 2, 1))
        return zb * cos + partner * sin

    def split_heads(blk):
        swapped = pltpu.roll(blk, HALF_LANES, 1)
        return ((jnp.where(low_half, blk, 0.0), jnp.where(low_half, 0.0, swapped)),
                (jnp.where(low_half, swapped, 0.0), jnp.where(low_half, 0.0, blk)))

    h_ext = jnp.concatenate(
        [_rms_scale(xp_ref[...], g_pre), hf, _rms_scale(xn_ref[...], g_pre)], axis=0).astype(jnp.bfloat16)
    ext = tm + 2 * POOL_HALO

    def u_task(c):
        def mm():
            return jnp.dot(h_ext, w_ref[:, C_U + c:C_U + c + COL_CHUNK], preferred_element_type=jnp.float32)

        def ep(z):
            keep_top = jnp.logical_not(is_first)
            keep_bottom = jnp.logical_not(is_last)
            uext[0:POOL_HALO, c:c + COL_CHUNK] = jnp.where(keep_top, z[0:POOL_HALO], 0.0)
            uext[POOL_HALO:ext - POOL_HALO, c:c + COL_CHUNK] = z[POOL_HALO:ext - POOL_HALO]
            uext[ext - POOL_HALO:ext, c:c + COL_CHUNK] = jnp.where(keep_bottom, z[ext - POOL_HALO:ext], 0.0)
            uext[ext:ext + POOL_PAD, c:c + COL_CHUNK] = jnp.zeros((POOL_PAD, COL_CHUNK), jnp.float32)
        return mm, ep

    pos = i * tm + lax.broadcasted_iota(jnp.int32, (tm, LANES), 0)
    pool_vals = {}

    def pooled_group(g, w):
        cols = slice(g * POOL_GROUP, (g + 1) * POOL_GROUP)
        u_all = uext[:, cols]
        center = u_all[POOL_HALO:POOL_HALO + tm]
        n1 = tm + 2 * POOL_HALO + SUBLANES
        win2 = u_all[0:n1] + u_all[1:n1 + 1]
        if w == 2:
            total = win2[POOL_HALO - 1:POOL_HALO - 1 + tm]
        elif w == 4:
            total = win2[POOL_HALO - 2:POOL_HALO - 2 + tm] + win2[POOL_HALO:POOL_HALO + tm]
        else:
            n2 = tm + 2 * POOL_HALO
            win4 = win2[0:n2] + win2[2:n2 + 2]
            if w == 8:
                total = win4[POOL_HALO - 4:POOL_HALO - 4 + tm] + win4[POOL_HALO:POOL_HALO + tm]
            else:
                n3 = tm + POOL_HALO
                win8 = win4[0:n3] + win4[4:n3 + 4]
                total = win8[0:tm] + win8[POOL_HALO:POOL_HALO + tm]
        lo = jnp.maximum(pos - w // 2, 0)
        hi = jnp.minimum(pos - w // 2 + w, seq)
        inv_cnt = 1.0 / (hi - lo).astype(jnp.float32)
        inv_cnt = jnp.concatenate([inv_cnt] * (POOL_GROUP // LANES), axis=-1)
        return (total * inv_cnt - center).astype(jnp.bfloat16)

    def pool_gate_task(g, w):
        def ep(z):
            pool_vals[g] = (_silu(z), pooled_group(g, w))
        return (lambda: proj(C_PG + g * POOL_GROUP, POOL_GROUP)), ep

    def pool_mix_task(g):
        cols = slice(g * POOL_GROUP, (g + 1) * POOL_GROUP)

        def mm():
            return jnp.dot(pool_vals[g][1], wpg_ref[g], preferred_element_type=jnp.float32)

        def ep(z):
            pb_ref[:, cols] = (z * ps_ref[:, cols] * pool_vals[g][0]).astype(pb_ref.dtype)
        return mm, ep

    q_scale = (HEAD_DIM ** -0.5) * LOG2_E

    def q_task(c):
        def ep(z):
            for j in range(COL_CHUNK // LANES):
                zb = rope(z[:, j * LANES:(j + 1) * LANES]) * q_scale
                q_ref[:, c + j * LANES:c + (j + 1) * LANES] = zb.astype(q_ref.dtype)
        return (lambda: proj(C_Q + c, COL_CHUNK)), ep

    def act_task(lo, out_ref, c, act):
        def ep(z):
            out_ref[:, c:c + COL_CHUNK] = act(z).astype(out_ref.dtype)
        return (lambda: proj(lo + c, COL_CHUNK)), ep

    def k_ep(zk):
        for j in range(KV_WIDTH // LANES):
            for par, (lowv, highv) in enumerate(split_heads(rope(zk[:, j * LANES:(j + 1) * LANES]))):
                base = (2 * j + par) * KV_SLOT
                kt_ref[base:base + LANES, :] = lowv.T.astype(kt_ref.dtype)
                kt_ref[base + LANES:base + KV_SLOT, :] = highv.T.astype(kt_ref.dtype)

    def v_ep(zv):
        ones_low = jnp.where(low_half, 1.0, 0.0).astype(vl_ref.dtype)
        ones_high = jnp.where(low_half, 0.0, 1.0).astype(vh_ref.dtype)
        for j in range(KV_WIDTH // LANES):
            for par, (lowv, highv) in enumerate(split_heads(zv[:, j * LANES:(j + 1) * LANES])):
                base = (2 * j + par) * KV_SLOT
                vl_ref[:, base:base + LANES] = lowv.astype(vl_ref.dtype)
                vl_ref[:, base + LANES:base + KV_SLOT] = ones_low
                vh_ref[:, base:base + LANES] = highv.astype(vh_ref.dtype)
                vh_ref[:, base + LANES:base + KV_SLOT] = ones_high

    tasks = [u_task(c) for c in range(0, POOL_WIDTH, COL_CHUNK)]
    tasks += [q_task(c) for c in range(0, ATTN_WIDTH, COL_CHUNK)]
    tasks += [pool_gate_task(g, w) for g, w in enumerate(POOL_WINDOWS)]
    tasks += [(lambda: proj(C_K, KV_WIDTH), k_ep), (lambda: proj(C_V, KV_WIDTH), v_ep)]
    tasks += [pool_mix_task(g) for g in range(N_POOL_GROUPS)]
    tasks += [act_task(C_AG, ag_ref, c, _silu) for c in range(0, ATTN_WIDTH, COL_CHUNK)]
    tasks += [act_task(C_MG, gate_ref, c, jax.nn.sigmoid) for c in range(0, 2 * D_MODEL, COL_CHUNK)]
    pending = None
    for mm, ep in tasks:
        z = mm()
        if pending is not None:
            pending[0](pending[1])
        pending = (ep, z)
    pending[0](pending[1])


def _inproj_call(x, g_pre, w_in, cos, sin, wpg, ps, *, tm):
    batch, seq, _ = x.shape
    nst = seq // tm
    bf16 = jnp.bfloat16
    kvw = N_KV_HEADS * KV_SLOT
    row_spec = lambda width: pl.BlockSpec((None, tm, width), lambda b, i: (b, i, 0))
    const = lambda shape: pl.BlockSpec(shape, lambda b, i: (0,) * len(shape))
    per = tm // POOL_HALO
    last = seq // POOL_HALO - 1
    out_shape = (
        jax.ShapeDtypeStruct((batch, seq, POOL_WIDTH), bf16),
        jax.ShapeDtypeStruct((batch, seq, ATTN_WIDTH), bf16),
        jax.ShapeDtypeStruct((batch, seq, ATTN_WIDTH), bf16),
        jax.ShapeDtypeStruct((batch, seq, 2 * D_MODEL), bf16),
        jax.ShapeDtypeStruct((batch, kvw, seq), bf16),
        jax.ShapeDtypeStruct((batch, seq, kvw), bf16),
        jax.ShapeDtypeStruct((batch, seq, kvw), bf16),
    )
    out_specs = (
        row_spec(POOL_WIDTH), row_spec(ATTN_WIDTH), row_spec(ATTN_WIDTH), row_spec(2 * D_MODEL),
        pl.BlockSpec((None, kvw, tm), lambda b, i: (b, 0, i)),
        row_spec(kvw), row_spec(kvw),
    )
    in_specs = [
        row_spec(D_MODEL),
        pl.BlockSpec((None, POOL_HALO, D_MODEL), lambda b, i: (b, jnp.maximum(i * per - 1, 0), 0)),
        pl.BlockSpec((None, POOL_HALO, D_MODEL), lambda b, i: (b, jnp.minimum((i + 1) * per, last), 0)),
        const((1, D_MODEL)),
        const((D_MODEL, IN_WIDTH)),
        pl.BlockSpec((tm, LANES), lambda b, i: (i, 0)),
        pl.BlockSpec((tm, LANES), lambda b, i: (i, 0)),
        const((N_POOL_GROUPS, POOL_GROUP, POOL_GROUP)),
        const((1, POOL_WIDTH)),
    ]
    return pl.pallas_call(
        functools.partial(_inproj_kernel, seq=seq),
        grid=(batch, nst),
        in_specs=in_specs,
        out_specs=out_specs,
        out_shape=out_shape,
        scratch_shapes=[pltpu.VMEM((tm + 2 * POOL_HALO + POOL_PAD, POOL_WIDTH), jnp.float32)],
        compiler_params=pltpu.CompilerParams(
            dimension_semantics=("arbitrary", "arbitrary"),
            vmem_limit_bytes=VMEM_LIMIT_BYTES),
        name="inproj",
    )(x, x, x, g_pre, w_in, cos, sin, wpg, ps)


def _mixer_kernel(q_ref, ag_ref,
                  kt_ref, ktp_ref, ktn_ref, vl_ref, vlp_ref, vln_ref, vh_ref, vhp_ref, vhn_ref,
                  sink_ref,
                  ab_ref,
                  kfull, vlfull, vhfull):
    tq = q_ref.shape[0]
    nqb = tq // BLOCK
    i = pl.program_id(1)
    is_first = i == 0
    is_last = i == pl.num_programs(1) - 1

    kfull[:, 0:WINDOW] = ktp_ref[...]
    kfull[:, WINDOW:WINDOW + tq] = kt_ref[...]
    kfull[:, WINDOW + tq:] = ktn_ref[...]
    for full, prev, cur, nxt in ((vlfull, vlp_ref, vl_ref, vln_ref), (vhfull, vhp_ref, vh_ref, vhn_ref)):
        full[0:WINDOW, :] = prev[...]
        full[WINDOW:WINDOW + tq, :] = cur[...]
        full[WINDOW + tq:, :] = nxt[...]

    qrow = lax.broadcasted_iota(jnp.int32, (BLOCK, LANES), 0)
    kcol = lax.broadcasted_iota(jnp.int32, (BLOCK, LANES), 1)
    low_half = kcol < HALF_LANES

    def chunk_masks(qi):
        first = kcol >= qrow
        last = kcol <= qrow
        if qi == 0:
            first = first & jnp.logical_not(is_first)
        if qi == nqb - 1:
            last = last & jnp.logical_not(is_last)
        return first, last

    masks = [chunk_masks(qi) for qi in range(nqb)]
    n_chunks = SPAN // LANES

    def scores(qi, hkv):
        rows = slice(qi * BLOCK, (qi + 1) * BLOCK)
        keys = slice(qi * BLOCK, qi * BLOCK + SPAN)
        q2 = jnp.concatenate([q_ref[rows, (2 * hkv + r) * LANES:(2 * hkv + r + 1) * LANES]
                              for r in range(GROUP // 2)], axis=0)
        kw = jnp.concatenate([kfull[hkv * KV_SLOT:hkv * KV_SLOT + LANES, keys],
                              kfull[hkv * KV_SLOT + LANES:(hkv + 1) * KV_SLOT, keys]], axis=1)
        return jnp.dot(q2, kw, preferred_element_type=jnp.float32)

    def softmax(qi, hkv, s):
        first_ok, last_ok = masks[qi]
        p_rows, sink_terms = [], []
        for r in range(GROUP // 2):
            p_cols, sink_pair = [], []
            for c in range(2):
                head = hkv * GROUP + 2 * r + c
                blk = s[r * BLOCK:(r + 1) * BLOCK, c * SPAN:(c + 1) * SPAN]
                chunks = [blk[:, k * LANES:(k + 1) * LANES] for k in range(n_chunks)]
                chunks[0] = jnp.where(first_ok, chunks[0], NEG_INF)
                chunks[-1] = jnp.where(last_ok, chunks[-1], NEG_INF)
                top = functools.reduce(jnp.maximum, chunks)
                sink_row = sink_ref[head:head + 1, :]
                m = jnp.maximum(jnp.max(top, axis=-1, keepdims=True), sink_row)
                p_cols += [jnp.exp2(ch - m).astype(jnp.bfloat16) for ch in chunks]
                sink_pair.append(jnp.exp2(sink_row - m))
            p_rows.append(jnp.concatenate(p_cols, axis=-1))
            sink_terms.append(jnp.where(low_half, sink_pair[0], sink_pair[1]))
        return jnp.concatenate(p_rows, axis=0), sink_terms

    def weighted_values(qi, hkv, p):
        keys = slice(qi * BLOCK, qi * BLOCK + SPAN)
        slot = slice(hkv * KV_SLOT, (hkv + 1) * KV_SLOT)
        vw = jnp.concatenate([vlfull[keys, slot], vhfull[keys, slot]], axis=0)
        return jnp.dot(p, vw, preferred_element_type=jnp.float32)

    def finalize(qi, hkv, o, sink_terms):
        rows = slice(qi * BLOCK, (qi + 1) * BLOCK)
        for r in range(GROUP // 2):
            lanes = slice((2 * hkv + r) * LANES, (2 * hkv + r + 1) * LANES)
            values = o[r * BLOCK:(r + 1) * BLOCK, 0:LANES]
            denom = o[r * BLOCK:(r + 1) * BLOCK, LANES:KV_SLOT] + sink_terms[r]
            gated = values / denom * ag_ref[rows, lanes].astype(jnp.float32)
            ab_ref[rows, lanes] = gated.astype(ab_ref.dtype)

    items = [(qi, hkv) for qi in range(nqb) for hkv in range(N_KV_HEADS)]

    n_steps = len(items) + LAG_FINALIZE
    s_q, p_q, o_q = {}, {}, {}
    for t in range(n_steps):
        if t < len(items):
            s_q[t] = scores(*items[t])
        a, b, c = t - LAG_SOFTMAX, t - LAG_VALUES, t - LAG_FINALIZE
        if 0 <= a < len(items):
            p_q[a] = softmax(*items[a], s_q.pop(a))
        if 0 <= b < len(items):
            p, sink_terms = p_q.pop(b)
            o_q[b] = (weighted_values(*items[b], p), sink_terms)
        if 0 <= c < len(items):
            finalize(*items[c], *o_q.pop(c))


def _merge_kernel(x_ref, pb_ref, ab_ref, gate_ref, wpp_ref, wap_ref, wo_ref, gpost_ref, y_ref):
    pp = jnp.dot(pb_ref[...], wpp_ref[...], preferred_element_type=jnp.float32)
    ap = jnp.dot(ab_ref[...], wap_ref[...], preferred_element_type=jnp.float32)
    merged = (gate_ref[:, :D_MODEL].astype(jnp.float32) * pp
              + gate_ref[:, D_MODEL:].astype(jnp.float32) * ap)
    out = jnp.dot(merged.astype(jnp.bfloat16), wo_ref[...], preferred_element_type=jnp.float32)
    y_ref[...] = x_ref[...] + _rms_scale(out, gpost_ref[...])


def _merge_call(x, pb, ab, gates, wpp, wap, wo, g_post, *, tr):
    batch, seq, _ = x.shape
    row_spec = lambda width: pl.BlockSpec((None, tr, width), lambda b, i: (b, i, 0))
    const = lambda shape: pl.BlockSpec(shape, lambda b, i: (0,) * len(shape))
    return pl.pallas_call(
        _merge_kernel,
        grid=(batch, seq // tr),
        in_specs=[row_spec(D_MODEL), row_spec(POOL_WIDTH), row_spec(ATTN_WIDTH), row_spec(2 * D_MODEL),
                  const((POOL_WIDTH, D_MODEL)), const((ATTN_WIDTH, D_MODEL)), const((D_MODEL, D_MODEL)),
                  const((1, D_MODEL))],
        out_specs=row_spec(D_MODEL),
        out_shape=jax.ShapeDtypeStruct((batch, seq, D_MODEL), jnp.float32),
        compiler_params=pltpu.CompilerParams(
            dimension_semantics=("arbitrary", "arbitrary"),
            vmem_limit_bytes=VMEM_LIMIT_BYTES),
        name="merge",
    )(x, pb, ab, gates, wpp, wap, wo, g_post)


def _mixer_call(q, ag, kt, vl, vh, sink, *, tq):
    batch, seq, _ = q.shape
    nt = seq // tq
    kvw = N_KV_HEADS * KV_SLOT
    row_spec = lambda width: pl.BlockSpec((None, tq, width), lambda b, i: (b, i, 0))
    const = lambda shape: pl.BlockSpec(shape, lambda b, i: (0,) * len(shape))
    per = tq // WINDOW
    last = seq // WINDOW - 1
    prev_blk = lambda i: jnp.maximum(i * per - 1, 0)
    next_blk = lambda i: jnp.minimum((i + 1) * per, last)
    v_specs = [row_spec(kvw),
               pl.BlockSpec((None, WINDOW, kvw), lambda b, i: (b, prev_blk(i), 0)),
               pl.BlockSpec((None, WINDOW, kvw), lambda b, i: (b, next_blk(i), 0))]
    in_specs = [
        row_spec(ATTN_WIDTH),
        row_spec(ATTN_WIDTH),
        pl.BlockSpec((None, kvw, tq), lambda b, i: (b, 0, i)),
        pl.BlockSpec((None, kvw, WINDOW), lambda b, i: (b, 0, prev_blk(i))),
        pl.BlockSpec((None, kvw, WINDOW), lambda b, i: (b, 0, next_blk(i))),
        *v_specs, *v_specs,
        const((N_HEADS, LANES)),
    ]
    scratch_shapes = [
        pltpu.VMEM((kvw, tq + 2 * WINDOW), jnp.bfloat16),
        pltpu.VMEM((tq + 2 * WINDOW, kvw), jnp.bfloat16),
        pltpu.VMEM((tq + 2 * WINDOW, kvw), jnp.bfloat16),
    ]
    return pl.pallas_call(
        _mixer_kernel,
        grid=(batch, nt),
        in_specs=in_specs,
        out_specs=row_spec(ATTN_WIDTH),
        out_shape=jax.ShapeDtypeStruct((batch, seq, ATTN_WIDTH), jnp.bfloat16),
        scratch_shapes=scratch_shapes,
        compiler_params=pltpu.CompilerParams(
            dimension_semantics=("arbitrary", "arbitrary"),
            vmem_limit_bytes=VMEM_LIMIT_BYTES),
        name="mixer",
    )(q, ag, kt, kt, kt, vl, vl, vl, vh, vh, vh, sink)


def _rope_tables(seq):
    half = HEAD_DIM // 2
    inv = ROPE_THETA ** (-jnp.arange(half, dtype=jnp.float32) / half)
    ang = jnp.arange(seq).astype(jnp.float32)[:, None] * inv[None, :]
    cos = jnp.cos(ang)
    sin = jnp.sin(ang)
    reps = LANES // HEAD_DIM
    cos_t = jnp.tile(jnp.concatenate([cos, cos], axis=-1), (1, reps))
    sin_t = jnp.tile(jnp.concatenate([-sin, sin], axis=-1), (1, reps))
    return cos_t, sin_t


def _encoder_layer(x, norm_pre, w_in, w_pool_group, pool_scale, w_pool_proj,
                   attn_sink, w_attn_proj, w_out, norm_post, *, tm=TM_INPROJ, tq=TQ_MIX):
    batch, seq, _ = x.shape
    tm = min(tm, seq)
    tq = min(tq, seq)
    assert seq % tm == 0 and seq % tq == 0 and tq % BLOCK == 0 and tm % LANES == 0
    bf16 = jnp.bfloat16
    cos_t, sin_t = _rope_tables(seq)
    pb, q, ag, gates, kt, vl, vh = _inproj_call(
        x, norm_pre.reshape(1, D_MODEL), w_in.astype(bf16), cos_t, sin_t,
        w_pool_group.astype(bf16), pool_scale.reshape(1, POOL_WIDTH).astype(jnp.float32), tm=tm)
    sink = jnp.broadcast_to((attn_sink.astype(jnp.float32) * LOG2_E)[:, None], (N_HEADS, LANES))
    ab = _mixer_call(q, ag, kt, vl, vh, sink, tq=tq)
    return _merge_call(
        x, pb, ab, gates, w_pool_proj.astype(bf16), w_attn_proj.astype(bf16), w_out.astype(bf16),
        norm_post.reshape(1, D_MODEL), tr=tq)


def kernel(x_prompt, x_sample, norm_pre, w_in, w_pool_group, pool_scale, w_pool_proj,
           attn_sink, w_attn_proj, w_out, norm_post):
    y_prompt, y_sample = x_prompt, x_sample
    for l in range(norm_pre.shape[0]):
        layer = functools.partial(
            _encoder_layer, norm_pre=norm_pre[l], w_in=w_in[l], w_pool_group=w_pool_group[l],
            pool_scale=pool_scale[l], w_pool_proj=w_pool_proj[l], attn_sink=attn_sink[l],
            w_attn_proj=w_attn_proj[l], w_out=w_out[l], norm_post=norm_post[l])
        y_prompt = layer(y_prompt)
        y_sample = layer(y_sample)
    return (y_prompt, y_sample)
```
